```python
import math
import jax, jax.numpy as jnp
from jax import lax
import numpy as np


D_MODEL = 1024
BATCH = 4
SEQ = 8192
DEPTH = 2

GRID_W = 64
CTX_LEN = 256
DA_HEADS = 4
DA_HEAD_DIM = 64
DA_QK_W = DA_HEADS * 2 * DA_HEAD_DIM
DA_V_W = DA_HEADS * 2 * DA_HEAD_DIM
CONV_C = 512
CONV_K = 31
GATE_W = 2 * D_MODEL
Q_END = DA_QK_W
K_END = Q_END + DA_QK_W
V_END = K_END + DA_V_W
U_END = V_END + 2 * CONV_C
IN_W = U_END + GATE_W
ROPE_AXIS_DIM = DA_HEAD_DIM // 2
ROPE_BASE = 10000.0
Q_BLOCK = 128
N_EXPERTS = 16
N_GROUPS = 4
EXPERTS_PER_GROUP = N_EXPERTS // N_GROUPS
TOP_K = 2
D_EXPERT = 512
N_MOD = 6
DEEPNORM_ALPHA = (2 * DEPTH) ** 0.25
DEEPNORM_BETA = (8 * DEPTH) ** -0.25
EPS = 1e-5

kernel_name = 'hybrid_diffattn_conformer_grouped_moe_dit'


def layer_norm(x, g, b):
    xf = x.astype(jnp.float32)
    mu = jnp.mean(xf, -1, keepdims=True)
    var = jnp.mean(jnp.square(xf - mu), -1, keepdims=True)
    return ((xf - mu) * lax.rsqrt(var + EPS) * g + b).astype(x.dtype)


def modulate(x, shift, scale):
    return x * (1.0 + scale) + shift


def axial_rope(rows):
    row = jnp.repeat(jnp.arange(rows, dtype=jnp.float32), GRID_W)
    col = jnp.tile(jnp.arange(GRID_W, dtype=jnp.float32), rows)
    inv_freq = ROPE_BASE ** (-jnp.arange(0, ROPE_AXIS_DIM, 2, dtype=jnp.float32) / ROPE_AXIS_DIM)
    ang = jnp.concatenate([row[:, None] * inv_freq, col[:, None] * inv_freq], -1)
    return jnp.cos(ang), jnp.sin(ang)


def apply_rope(x, cos, sin):
    xp = x.astype(jnp.float32).reshape(*x.shape[:-1], DA_HEAD_DIM // 2, 2)
    c = cos[None, :, None, None, :]
    s = sin[None, :, None, None, :]
    x0, x1 = xp[..., 0], xp[..., 1]
    out = jnp.stack([x0 * c - x1 * s, x0 * s + x1 * c], -1)
    return out.reshape(x.shape).astype(x.dtype)


def qk_heads(p):
    return p.reshape(*p.shape[:-1], DA_HEADS, 2, DA_HEAD_DIM)


def v_heads(p):
    return p.reshape(*p.shape[:-1], DA_HEADS, 2 * DA_HEAD_DIM)


def diff_attend(q, k, v, lam):
    s = jnp.einsum('bqhmd,bkhmd->bhmqk', q, k).astype(jnp.float32) * (DA_HEAD_DIM ** -0.5)
    a = jax.nn.softmax(s, axis=-1)
    a = a[:, :, 0] - lam * a[:, :, 1]
    return jnp.einsum('bhqk,bkhe->bqhe', a.astype(v.dtype), v)


def latent_diff_attention(q, k, v, lam):
    b, n = q.shape[:2]
    n_blocks = n // Q_BLOCK
    q_blocks = jnp.moveaxis(q.reshape(b, n_blocks, Q_BLOCK, *q.shape[2:]), 1, 0)
    out = lax.map(lambda qb: diff_attend(qb, k, v, lam), q_blocks)
    return jnp.moveaxis(out, 0, 1).reshape(b, n, DA_HEADS, 2 * DA_HEAD_DIM)


def diff_heads_out(o, g, lam_init, w_o):
    of = o.astype(jnp.float32)
    of = of * lax.rsqrt(jnp.mean(jnp.square(of), -1, keepdims=True) + EPS) * g * (1.0 - lam_init)
    return of.reshape(*o.shape[:-2], DA_V_W).astype(o.dtype) @ w_o


def conformer_conv(u, conv_w, conv_b, ln_g, ln_b, w_o):
    a, gate = jnp.split(u, 2, axis=-1)
    y = a * jax.nn.sigmoid(gate)
    y = lax.conv_general_dilated(
        y, conv_w[:, None, :], window_strides=(1,),
        padding=((CONV_K // 2, CONV_K // 2),),
        dimension_numbers=('NWC', 'WIO', 'NWC'),
        feature_group_count=CONV_C) + conv_b
    y = jax.nn.silu(layer_norm(y, ln_g, ln_b))
    return y @ w_o


def mixer_merge(p, attn, lam_init, subln_g, w_attn_o, conv_w, conv_b, conv_ln_g, conv_ln_b, w_conv_o, w_out):
    att = diff_heads_out(attn, subln_g, lam_init, w_attn_o)
    cnv = conformer_conv(p[..., V_END:U_END], conv_w, conv_b, conv_ln_g, conv_ln_b, w_conv_o)
    g_att, g_cnv = jnp.split(p[..., U_END:], 2, axis=-1)
    return (jax.nn.sigmoid(g_att) * att + jax.nn.sigmoid(g_cnv) * cnv) @ w_out


def grouped_moe(h, w_router, b_router, w_gate, w_up, w_down):
    probs = jax.nn.softmax((h @ w_router + b_router).astype(jnp.float32), axis=-1)
    grouped = probs.reshape(*probs.shape[:-1], N_GROUPS, EXPERTS_PER_GROUP)
    group_score = lax.top_k(grouped, TOP_K)[0].sum(-1)
    group_sel = jnp.argmax(group_score, -1)[..., None] == jnp.arange(N_GROUPS)
    expert_ok = jnp.repeat(group_sel, EXPERTS_PER_GROUP, axis=-1)
    top_p, top_i = lax.top_k(jnp.where(expert_ok, probs, -1.0), TOP_K)
    top_w = top_p / top_p.sum(-1, keepdims=True)
    gates = jnp.einsum('...ke,...k->...e', jax.nn.one_hot(top_i, N_EXPERTS, dtype=jnp.float32), top_w).astype(h.dtype)
    out = jnp.zeros_like(h)
    for e in range(N_EXPERTS):
        y = (jax.nn.silu(h @ w_gate[e]) * (h @ w_up[e])) @ w_down[e]
        out = out + gates[..., e:e + 1] * y
    return out


def setup_inputs(seed: int = 0) -> dict:
    key = jax.random.key(seed)
    ks = jax.random.split(key, 28)

    def nrm(i, shape, scale):
        return scale * jax.random.normal(ks[i], shape, jnp.float32)

    D, L = D_MODEL, DEPTH
    return {
        'x': nrm(0, (BATCH, SEQ, D), 1.0),
        'c': nrm(1, (BATCH, D), 1.0),
        'ctx': nrm(2, (BATCH, CTX_LEN, D), 1.0),
        'c_ctx': nrm(3, (D,), 1.0),
        'w_ada': nrm(4, (L, D, N_MOD * D), 0.5 * D ** -0.5),
        'b_ada': nrm(5, (L, N_MOD * D), 0.02),
        'w_in': nrm(6, (L, D, IN_W), D ** -0.5),
        'lam_q1': nrm(7, (L, DA_HEAD_DIM), 0.1),
        'lam_k1': nrm(8, (L, DA_HEAD_DIM), 0.1),
        'lam_q2': nrm(9, (L, DA_HEAD_DIM), 0.1),
        'lam_k2': nrm(10, (L, DA_HEAD_DIM), 0.1),
        'subln_g': 1.0 + nrm(11, (L, 2 * DA_HEAD_DIM), 0.02),
        'w_attn_o': nrm(12, (L, DA_V_W, D), DA_V_W ** -0.5),
        'conv_w': nrm(13, (L, CONV_K, CONV_C), CONV_K ** -0.5),
        'conv_b': nrm(14, (L, CONV_C), 0.02),
        'conv_ln_g': 1.0 + nrm(15, (L, CONV_C), 0.02),
        'conv_ln_b': nrm(16, (L, CONV_C), 0.02),
        'w_conv_o': nrm(17, (L, CONV_C, D), CONV_C ** -0.5),
        'w_out': nrm(18, (L, D, D), DEEPNORM_BETA * D ** -0.5),
        'ln1_g': 1.0 + nrm(19, (L, D), 0.02),
        'ln1_b': nrm(20, (L, D), 0.02),
        'w_router': nrm(21, (D, N_EXPERTS), D ** -0.5),
        'b_router': nrm(22, (N_EXPERTS,), 0.01),
        'w_e_gate': nrm(23, (L, N_EXPERTS, D, D_EXPERT), D ** -0.5),
        'w_e_up': nrm(24, (L, N_EXPERTS, D, D_EXPERT), D ** -0.5),
        'w_e_down': nrm(25, (L, N_EXPERTS, D_EXPERT, D), DEEPNORM_BETA * D_EXPERT ** -0.5),
        'ln2_g': 1.0 + nrm(26, (L, D), 0.02),
        'ln2_b': nrm(27, (L, D), 0.02),
    }


def reference(x, c, ctx, c_ctx, w_ada, b_ada, w_in, lam_q1, lam_k1, lam_q2, lam_k2, subln_g,
              w_attn_o, conv_w, conv_b, conv_ln_g, conv_ln_b, w_conv_o, w_out, ln1_g, ln1_b,
              w_router, b_router, w_e_gate, w_e_up, w_e_down, ln2_g, ln2_b):
    n = x.shape[1]
    ROWS = n // GRID_W
    cos, sin = axial_rope(ROWS)
    cx = ctx
    for l in range(DEPTH):
        last = l == DEPTH - 1
        lam_init = 0.8 - 0.6 * math.exp(-0.3 * l)
        lam = (jnp.exp(jnp.sum(lam_q1[l] * lam_k1[l]).astype(jnp.float32))
               - jnp.exp(jnp.sum(lam_q2[l] * lam_k2[l]).astype(jnp.float32)) + lam_init)
        mod_x = jnp.split((jax.nn.silu(c) @ w_ada[l] + b_ada[l])[:, None, :], N_MOD, axis=-1)
        mod_c = jnp.split(jax.nn.silu(c_ctx) @ w_ada[l] + b_ada[l], N_MOD, axis=-1)
        w_in_l = w_in[l]
        mix_args = (lam_init, subln_g[l], w_attn_o[l], conv_w[l], conv_b[l], conv_ln_g[l],
                    conv_ln_b[l], w_conv_o[l], w_out[l])
        moe_args = (w_router, b_router, w_e_gate[l], w_e_up[l], w_e_down[l])

        hx = modulate(x, mod_x[0], mod_x[1])
        hc = modulate(cx, mod_c[0], mod_c[1])
        px = hx @ w_in_l
        pc = hc @ (w_in_l[:, :V_END] if last else w_in_l)
        kc = qk_heads(pc[..., Q_END:K_END])
        vc = v_heads(pc[..., K_END:V_END])
        qx = apply_rope(qk_heads(px[..., :Q_END]), cos, sin)
        kx = apply_rope(qk_heads(px[..., Q_END:K_END]), cos, sin)
        vx = v_heads(px[..., K_END:V_END])
        attn_x = latent_diff_attention(qx, jnp.concatenate([kc, kx], 1),
                                       jnp.concatenate([vc, vx], 1), lam)
        mix_x = mixer_merge(px, attn_x, *mix_args)

        if not last:
            attn_c = diff_attend(qk_heads(pc[..., :Q_END]), kc, vc, lam)
            mix_c = mixer_merge(pc, attn_c, *mix_args)
            cx = layer_norm(DEEPNORM_ALPHA * cx + mod_c[2] * mix_c, ln1_g[l], ln1_b[l])
            hc = modulate(cx, mod_c[3], mod_c[4])
            cx = layer_norm(DEEPNORM_ALPHA * cx + mod_c[5] * grouped_moe(hc, *moe_args),
                            ln2_g[l], ln2_b[l])

        x = layer_norm(DEEPNORM_ALPHA * x + mod_x[2] * mix_x, ln1_g[l], ln1_b[l])
        hx = modulate(x, mod_x[3], mod_x[4])
        x = layer_norm(DEEPNORM_ALPHA * x + mod_x[5] * grouped_moe(hx, *moe_args),
                       ln2_g[l], ln2_b[l])
    return x
```

```python
import functools
import math

import jax
import jax.numpy as jnp
from jax import lax
from jax.experimental import pallas as pl
from jax.experimental.pallas import tpu as pltpu

F32 = jnp.float32
BF16 = jnp.bfloat16

D_MODEL = 1024
HEADS = 4
HEAD_DIM = 64
HEAD_W = 2 * HEAD_DIM
QK_W = HEADS * HEAD_W
CONV_C = 512
CONV_K = 31
CONV_HALO = 16
N_EXPERTS = 16
N_GROUPS = 4
EXPERTS_PER_GROUP = 4
PAIRS_PER_GROUP = 6
N_CLASSES = N_GROUPS * PAIRS_PER_GROUP
CLASS_ROWS = 32
D_EXPERT = 512
N_MOD = 6
GRID_W = 64
ROPE_AXIS_DIM = HEAD_DIM // 2
ROPE_BASE = 10000.0
EPS = 1e-5
LANES = 128
META_W = LANES
NEG_BIG = -1e30

VMEM_LIMIT = 56 * 1024 * 1024


def _dot(a, b):
    return jnp.dot(a, b, preferred_element_type=F32)


def _dot_nt(a, b):
    return lax.dot_general(a, b, (((1,), (1,)), ((), ())), preferred_element_type=F32)


def _split_bf16(a):
    hi = a.astype(BF16)
    lo = (a - hi.astype(F32)).astype(BF16)
    return hi, lo


def _sigmoid(x):
    return 1.0 / (1.0 + jnp.exp(-x))


def _layer_norm(x, g, b):
    mu = jnp.mean(x, axis=-1, keepdims=True)
    xc = x - mu
    var = jnp.mean(xc * xc, axis=-1, keepdims=True)
    return xc * lax.rsqrt(var + EPS) * g + b


def _params(sem):
    return pltpu.CompilerParams(dimension_semantics=sem, vmem_limit_bytes=VMEM_LIMIT)


def _ada_kernel(c_ref, w_ref, b_ref, o_ref):
    c = c_ref[...]
    a = c * _sigmoid(c)
    a_hi, a_lo = _split_bf16(a)
    w_hi, w_lo = _split_bf16(w_ref[...])
    o_ref[...] = _dot(a_hi, w_hi) + _dot(a_lo, w_hi) + _dot(a_hi, w_lo) + b_ref[...]


def _ada(cs, w, b):
    n = w.shape[1]
    nb = 1536
    return pl.pallas_call(
        _ada_kernel,
        grid=(n // nb,),
        in_specs=[pl.BlockSpec((8, D_MODEL), lambda j: (0, 0)),
                  pl.BlockSpec((D_MODEL, nb), lambda j: (0, j)),
                  pl.BlockSpec((1, nb), lambda j: (0, j))],
        out_specs=pl.BlockSpec((8, nb), lambda j: (0, j)),
        out_shape=jax.ShapeDtypeStruct((8, n), F32),
        compiler_params=_params(("arbitrary",)),
        name="ada",
    )(cs, w, b)


def _inproj_kernel(x_ref, mod_ref, cos_ref, sin_ref, wqk_ref, wvt_ref, wu_ref, wg_ref,
                   q_ref, k1_ref, k2_ref, vt_ref, y_ref, g_ref):
    m = mod_ref[0]
    hx = (x_ref[...] * (1.0 + m[1:2]) + m[0:1]).astype(BF16)

    qk = _dot(hx, wqk_ref[...])
    w = qk.shape[1]
    lane = lax.broadcasted_iota(jnp.int32, qk.shape, 1)
    first_half = (lane % HEAD_DIM) < (HEAD_DIM // 2)
    partner = jnp.where(first_half,
                        pltpu.roll(qk, w - HEAD_DIM // 2, 1),
                        pltpu.roll(qk, HEAD_DIM // 2, 1))
    reps = w // LANES
    cos = jnp.concatenate([cos_ref[...]] * reps, axis=1)
    sin = jnp.concatenate([sin_ref[...]] * reps, axis=1)
    roped = qk * cos + partner * sin
    q_ref[...] = (roped[:, :QK_W] * (HEAD_DIM ** -0.5)).astype(BF16)
    k = roped[:, QK_W:]
    map1 = (lax.broadcasted_iota(jnp.int32, k.shape, 1) % HEAD_W) < HEAD_DIM
    k1_ref[...] = jnp.where(map1, k, 0.0).astype(BF16)
    k2_ref[...] = jnp.where(map1, 0.0, k).astype(BF16)

    vt_ref[...] = _dot_nt(wvt_ref[...], hx).astype(BF16)

    u = _dot(hx, wu_ref[...])
    y_ref[...] = u[:, :CONV_C] * _sigmoid(u[:, CONV_C:])
    g_ref[...] = _sigmoid(_dot(hx, wg_ref[...])).astype(BF16)


def _inproj(x_all, mod, cos_t, sin_t, wqk, wvt, wu, wg, *, rb, n_lat, seq, batch):
    t = x_all.shape[0]
    nlat = n_lat // rb
    per_seq = seq // rb

    def mod_idx(i):
        return (jnp.where(i < nlat, (i * rb) // seq, batch), 0, 0)

    def rope_idx(i):
        return (jnp.where(i < nlat, i % per_seq, per_seq), 0)

    const = lambda i: (0, 0)
    row = lambda i: (i, 0)
    return pl.pallas_call(
        _inproj_kernel,
        grid=(t // rb,),
        in_specs=[pl.BlockSpec((rb, D_MODEL), row),
                  pl.BlockSpec((1, N_MOD, D_MODEL), mod_idx),
                  pl.BlockSpec((rb, LANES), rope_idx),
                  pl.BlockSpec((rb, LANES), rope_idx),
                  pl.BlockSpec(wqk.shape, const),
                  pl.BlockSpec(wvt.shape, const),
                  pl.BlockSpec(wu.shape, const),
                  pl.BlockSpec(wg.shape, const)],
        out_specs=[pl.BlockSpec((rb, QK_W), row),
                   pl.BlockSpec((rb, QK_W), row),
                   pl.BlockSpec((rb, QK_W), row),
                   pl.BlockSpec((QK_W, rb), lambda i: (0, i)),
                   pl.BlockSpec((rb, CONV_C), row),
                   pl.BlockSpec((rb, 2 * D_MODEL), row)],
        out_shape=[jax.ShapeDtypeStruct((t, QK_W), BF16),
                   jax.ShapeDtypeStruct((t, QK_W), BF16),
                   jax.ShapeDtypeStruct((t, QK_W), BF16),
                   jax.ShapeDtypeStruct((QK_W, t), BF16),
                   jax.ShapeDtypeStruct((t, CONV_C), F32),
                   jax.ShapeDtypeStruct((t, 2 * D_MODEL), BF16)],
        compiler_params=_params(("arbitrary",)),
        name="inproj",
    )(x_all, mod, cos_t, sin_t, wqk, wvt, wu, wg)


def _attn_kernel(*refs, lam_init, n_chunks, kc):
    if n_chunks:
        (q_ref, kc1_ref, kc2_ref, vtc_ref, k1_ref, k2_ref, vt_ref, lam_ref, g_ref,
         o_ref, m_ref, l_ref, acc_ref) = refs
    else:
        (q_ref, kc1_ref, kc2_ref, vtc_ref, lam_ref, g_ref, o_ref, m_ref, l_ref, acc_ref) = refs

    m_ref[...] = jnp.full(m_ref.shape, NEG_BIG, F32)
    l_ref[...] = jnp.zeros(l_ref.shape, F32)
    acc_ref[...] = jnp.zeros(acc_ref.shape, F32)
    q = q_ref[...]

    def step(keys, vt):
        for i, kk in enumerate(keys):
            s = _dot_nt(kk, q)
            m_old = m_ref[i]
            m_new = jnp.maximum(m_old, jnp.max(s, axis=0, keepdims=True))
            p = jnp.exp(s - m_new)
            alpha = jnp.exp(m_old - m_new)
            l_ref[i] = alpha * l_ref[i] + jnp.sum(p, axis=0, keepdims=True)
            acc_ref[i] = alpha * acc_ref[i] + _dot(vt, p.astype(BF16))
            m_ref[i] = m_new

    step((kc1_ref[...], kc2_ref[...]), vtc_ref[...])

    if n_chunks:
        def body(c, carry):
            off = pl.multiple_of(c * kc, kc)
            step((k1_ref[pl.ds(off, kc), :], k2_ref[pl.ds(off, kc), :]), vt_ref[:, pl.ds(off, kc)])
            return carry
        lax.fori_loop(0, n_chunks, body, 0)

    lv = lam_ref[...]
    lam = (jnp.exp(jnp.sum(lv[0:1] * lv[1:2], axis=1, keepdims=True))
           - jnp.exp(jnp.sum(lv[2:3] * lv[3:4], axis=1, keepdims=True)) + lam_init)
    o_t = acc_ref[0] / l_ref[0] - lam * (acc_ref[1] / l_ref[1])
    ms = jnp.mean(o_t * o_t, axis=0, keepdims=True)
    o_t = o_t * lax.rsqrt(ms + EPS) * (g_ref[...] * (1.0 - lam_init))
    o_ref[...] = o_t.T.astype(BF16)


def _attn_scratch(qb):
    return [pltpu.VMEM((2, 1, qb), F32), pltpu.VMEM((2, 1, qb), F32), pltpu.VMEM((2, HEAD_W, qb), F32)]


def _attn_latent(q, k1, k2, vt, lamv, g_col, *, lam_init, batch, seq, ctx, qb, kc):
    t = q.shape[0]
    n_lat = batch * seq
    nq = seq // qb
    ctx0 = n_lat // ctx
    kern = functools.partial(_attn_kernel, lam_init=lam_init, n_chunks=seq // kc, kc=kc)
    return pl.pallas_call(
        kern,
        grid=(batch, HEADS, nq),
        in_specs=[pl.BlockSpec((qb, HEAD_W), lambda b, h, i: (b * nq + i, h)),
                  pl.BlockSpec((ctx, HEAD_W), lambda b, h, i: (ctx0 + b, h)),
                  pl.BlockSpec((ctx, HEAD_W), lambda b, h, i: (ctx0 + b, h)),
                  pl.BlockSpec((HEAD_W, ctx), lambda b, h, i: (h, ctx0 + b)),
                  pl.BlockSpec((seq, HEAD_W), lambda b, h, i: (b, h)),
                  pl.BlockSpec((seq, HEAD_W), lambda b, h, i: (b, h)),
                  pl.BlockSpec((HEAD_W, seq), lambda b, h, i: (h, b)),
                  pl.BlockSpec((4, HEAD_DIM), lambda b, h, i: (0, 0)),
                  pl.BlockSpec((HEAD_W, 1), lambda b, h, i: (0, 0))],
        out_specs=pl.BlockSpec((qb, HEAD_W), lambda b, h, i: (b * nq + i, h)),
        out_shape=jax.ShapeDtypeStruct((t, QK_W), BF16),
        scratch_shapes=_attn_scratch(qb),
        compiler_params=_params(("arbitrary", "arbitrary", "arbitrary")),
        name="attn_latent",
    )(q, k1, k2, vt, k1, k2, vt, lamv, g_col)


def _attn_ctx(o_prev, q, k1, k2, vt, lamv, g_col, *, lam_init, batch, seq, ctx):
    t = q.shape[0]
    ctx0 = batch * seq // ctx

    def kern(o_prev_ref, *refs):
        del o_prev_ref
        _attn_kernel(*refs, lam_init=lam_init, n_chunks=0, kc=0)

    return pl.pallas_call(
        kern,
        grid=(batch, HEADS),
        in_specs=[pl.BlockSpec(memory_space=pl.ANY),
                  pl.BlockSpec((ctx, HEAD_W), lambda b, h: (ctx0 + b, h)),
                  pl.BlockSpec((ctx, HEAD_W), lambda b, h: (ctx0 + b, h)),
                  pl.BlockSpec((ctx, HEAD_W), lambda b, h: (ctx0 + b, h)),
                  pl.BlockSpec((HEAD_W, ctx), lambda b, h: (h, ctx0 + b)),
                  pl.BlockSpec((4, HEAD_DIM), lambda b, h: (0, 0)),
                  pl.BlockSpec((HEAD_W, 1), lambda b, h: (0, 0))],
        out_specs=pl.BlockSpec((ctx, HEAD_W), lambda b, h: (ctx0 + b, h)),
        out_shape=jax.ShapeDtypeStruct((t, QK_W), BF16),
        scratch_shapes=_attn_scratch(ctx),
        input_output_aliases={0: 0},
        compiler_params=_params(("arbitrary", "arbitrary")),
        name="attn_ctx",
    )(o_prev, q, k1, k2, vt, lamv, g_col)


CONV_ROW_TILE = 32


def _conv_kernel(y_ref, prev_ref, next_ref, w_ref, b_ref, g_ref, beta_ref, z_ref, buf_ref, *,
                 rb, nlat, per_seq):
    i = pl.program_id(0)
    pos = i % per_seq
    first = jnp.logical_or(i >= nlat, pos == 0)
    last = jnp.logical_or(i >= nlat, pos == per_seq - 1)

    buf_ref[CONV_HALO:CONV_HALO + rb, :] = y_ref[...]

    @pl.when(first)
    def _():
        buf_ref[0:CONV_HALO, :] = jnp.zeros((CONV_HALO, CONV_C), F32)

    @pl.when(jnp.logical_not(first))
    def _():
        buf_ref[0:CONV_HALO, :] = prev_ref[...]

    @pl.when(last)
    def _():
        buf_ref[CONV_HALO + rb:, :] = jnp.zeros((CONV_HALO, CONV_C), F32)

    @pl.when(jnp.logical_not(last))
    def _():
        buf_ref[CONV_HALO + rb:, :] = next_ref[...]

    base = CONV_HALO - CONV_K // 2
    for r0 in range(0, rb, CONV_ROW_TILE):
        acc = jnp.zeros((CONV_ROW_TILE, CONV_C), F32) + b_ref[...]
        for k in range(CONV_K):
            acc = acc + w_ref[k:k + 1, :] * buf_ref[base + r0 + k:base + r0 + k + CONV_ROW_TILE, :]
        v = _layer_norm(acc, g_ref[...], beta_ref[...])
        z_ref[r0:r0 + CONV_ROW_TILE, :] = (v * _sigmoid(v)).astype(BF16)


def _conv(y, w, b, g, beta, *, n_rows, n_lat, seq, ctx):
    rb = ctx
    hb = rb // CONV_HALO
    t = y.shape[0]
    n_halo = t // CONV_HALO
    kern = functools.partial(_conv_kernel, rb=rb, nlat=n_lat // rb, per_seq=seq // rb)
    const = lambda i: (0, 0)
    return pl.pallas_call(
        kern,
        grid=(n_rows // rb,),
        in_specs=[pl.BlockSpec((rb, CONV_C), lambda i: (i, 0)),
                  pl.BlockSpec((CONV_HALO, CONV_C), lambda i: (jnp.maximum(i * hb - 1, 0), 0)),
                  pl.BlockSpec((CONV_HALO, CONV_C), lambda i: (jnp.minimum((i + 1) * hb, n_halo - 1), 0)),
                  pl.BlockSpec(w.shape, const),
                  pl.BlockSpec((1, CONV_C), const),
                  pl.BlockSpec((1, CONV_C), const),
                  pl.BlockSpec((1, CONV_C), const)],
        out_specs=pl.BlockSpec((rb, CONV_C), lambda i: (i, 0)),
        out_shape=jax.ShapeDtypeStruct((t, CONV_C), BF16),
        scratch_shapes=[pltpu.VMEM((rb + 2 * CONV_HALO, CONV_C), F32)],
        compiler_params=_params(("arbitrary",)),
        name="conv",
    )(y, y, y, w, b, g, beta)


def _first_index(vals, target):
    idx = jnp.full(target.shape, len(vals) - 1, jnp.int32)
    for j in range(len(vals) - 2, -1, -1):
        idx = jnp.where(vals[j] == target, j, idx)
    return idx


def _route(logits_t):
    mx = jnp.max(logits_t, axis=0, keepdims=True)
    e = jnp.exp(logits_t - mx)
    m1s, m2s, i1s, i2s, scores = [], [], [], [], []
    for g in range(N_GROUPS):
        a = [e[g * EXPERTS_PER_GROUP + j:g * EXPERTS_PER_GROUP + j + 1, :] for j in range(EXPERTS_PER_GROUP)]
        m1 = functools.reduce(jnp.maximum, a)
        i1 = _first_index(a, m1)
        rest = [jnp.where(i1 == j, -1.0, a[j]) for j in range(EXPERTS_PER_GROUP)]
        m2 = functools.reduce(jnp.maximum, rest)
        i2 = _first_index(rest, m2)
        m1s.append(m1); m2s.append(m2); i1s.append(i1); i2s.append(i2); scores.append(m1 + m2)
    gstar = _first_index(scores, functools.reduce(jnp.maximum, scores))

    def pick(xs):
        out = xs[N_GROUPS - 1]
        for g in range(N_GROUPS - 2, -1, -1):
            out = jnp.where(gstar == g, xs[g], out)
        return out

    m1, m2, i1, i2 = pick(m1s), pick(m2s), pick(i1s), pick(i2s)
    tot = m1 + m2
    w1, w2 = m1 / tot, m2 / tot
    lo = jnp.minimum(i1, i2)
    hi = jnp.maximum(i1, i2)
    w_lo = jnp.where(i1 < i2, w1, w2)
    w_hi = jnp.where(i1 < i2, w2, w1)
    pair = jnp.where(lo == 0, 0, jnp.where(lo == 1, 3, 5)) + (hi - lo - 1)
    return gstar * PAIRS_PER_GROUP + pair, w_lo, w_hi


def _merge_kernel(o_ref, z_ref, g_ref, x_ref, mod_ref, wao_ref, wco_ref, wout_ref, lng_ref, lnb_ref,
                  wr_hi_ref, wr_lo_ref, br_ref,
                  x1_ref, hext_ref, meta_ref, cnt_ref, carry_ref, *, alpha):
    i = pl.program_id(0)

    @pl.when(i == 0)
    def _():
        carry_ref[...] = jnp.zeros(carry_ref.shape, F32)

    m = mod_ref[0]
    att = _dot(o_ref[...], wao_ref[...])
    cnv = _dot(z_ref[...], wco_ref[...])
    gates = g_ref[...].astype(F32)
    mixin = gates[:, :D_MODEL] * att + gates[:, D_MODEL:] * cnv
    mix = _dot(mixin.astype(BF16), wout_ref[...])
    x1 = _layer_norm(alpha * x_ref[...] + m[2:3] * mix, lng_ref[...], lnb_ref[...])
    x1_ref[...] = x1
    h = x1 * (1.0 + m[4:5]) + m[3:4]
    hext_ref[:, :D_MODEL] = h

    h_hi, h_lo = _split_bf16(h)
    logits_t = (_dot_nt(wr_hi_ref[...], h_hi) + _dot_nt(wr_hi_ref[...], h_lo)
                + _dot_nt(wr_lo_ref[...], h_hi) + br_ref[...])
    cls, w_lo, w_hi = _route(logits_t)
    rb = h.shape[0]

    crow = lax.broadcasted_iota(jnp.int32, (CLASS_ROWS, rb), 0)
    onehot = (crow == cls).astype(F32)
    tri = (lax.broadcasted_iota(jnp.int32, (rb, rb), 0)
           < lax.broadcasted_iota(jnp.int32, (rb, rb), 1)).astype(BF16)
    before = _dot(onehot.astype(BF16), tri) + carry_ref[:, 0:1]
    rank = jnp.sum(onehot * before, axis=0, keepdims=True)
    carry_ref[...] = carry_ref[...] + jnp.sum(onehot, axis=1, keepdims=True)
    cnt_ref[...] = carry_ref[...]

    mrow = lax.broadcasted_iota(jnp.int32, (8, rb), 0)
    meta_ref[...] = jnp.where(mrow == 0, cls.astype(F32), jnp.where(mrow == 1, rank, 0.0))

    wrow = lax.broadcasted_iota(jnp.int32, (META_W, rb), 0)
    wts_t = jnp.where(wrow == 0, w_lo, jnp.where(wrow == 1, w_hi, 0.0))
    hext_ref[:, D_MODEL:] = wts_t.T


def _merge(o, z, gates, x_all, mod, wao, wco, wout, lng, lnb, wr_hi, wr_lo, br, *,
           rb, n_rows, n_lat, seq, batch, alpha):
    t = x_all.shape[0]
    nlat = n_lat // rb

    def mod_idx(i):
        return (jnp.where(i < nlat, (i * rb) // seq, batch), 0, 0)

    const = lambda i: (0, 0)
    row = lambda i: (i, 0)
    kern = functools.partial(_merge_kernel, alpha=alpha)
    return pl.pallas_call(
        kern,
        grid=(n_rows // rb,),
        in_specs=[pl.BlockSpec((rb, QK_W), row),
                  pl.BlockSpec((rb, CONV_C), row),
                  pl.BlockSpec((rb, 2 * D_MODEL), row),
                  pl.BlockSpec((rb, D_MODEL), row),
                  pl.BlockSpec((1, N_MOD, D_MODEL), mod_idx),
                  pl.BlockSpec(wao.shape, const),
                  pl.BlockSpec(wco.shape, const),
                  pl.BlockSpec(wout.shape, const),
                  pl.BlockSpec((1, D_MODEL), const),
                  pl.BlockSpec((1, D_MODEL), const),
                  pl.BlockSpec(wr_hi.shape, const),
                  pl.BlockSpec(wr_lo.shape, const),
                  pl.BlockSpec((N_EXPERTS, 1), const)],
        out_specs=[pl.BlockSpec((rb, D_MODEL), row),
                   pl.BlockSpec((rb, D_MODEL + META_W), row),
                   pl.BlockSpec((8, rb), lambda i: (0, i)),
                   pl.BlockSpec((CLASS_ROWS, LANES), const)],
        out_shape=[jax.ShapeDtypeStruct((t, D_MODEL), F32),
                   jax.ShapeDtypeStruct((t, D_MODEL + META_W), F32),
                   jax.ShapeDtypeStruct((8, t), F32),
                   jax.ShapeDtypeStruct((CLASS_ROWS, LANES), F32)],
        scratch_shapes=[pltpu.VMEM((CLASS_ROWS, LANES), F32)],
        compiler_params=_params(("arbitrary",)),
        name="merge",
    )(o, z, gates, x_all, mod, wao, wco, wout, lng, lnb, wr_hi, wr_lo, br)


def _dispatch_kernel(pos_ref, h_ref, init_ref, out_ref, sem, *, rb):
    del init_ref

    def row_copy(r, dst_row):
        return pltpu.make_async_copy(h_ref.at[pl.ds(r, 1)], out_ref.at[pl.ds(dst_row, 1)], sem)

    def issue(r, carry):
        row_copy(r, pos_ref[0, 0, r]).start()
        return carry

    lax.fori_loop(0, rb, issue, 0)

    def drain(r, carry):
        row_copy(r, 0).wait()
        return carry

    lax.fori_loop(0, rb, drain, 0)


def _dispatch(pos3, hext, init, *, rb, n_rows):
    width = hext.shape[1]
    kern = functools.partial(_dispatch_kernel, rb=rb)
    return pl.pallas_call(
        kern,
        grid=(n_rows // rb,),
        in_specs=[pl.BlockSpec((1, 1, rb), lambda i: (i, 0, 0), memory_space=pltpu.SMEM),
                  pl.BlockSpec((rb, width), lambda i: (i, 0)),
                  pl.BlockSpec(memory_space=pl.ANY)],
        out_specs=pl.BlockSpec(memory_space=pl.ANY),
        out_shape=jax.ShapeDtypeStruct(init.shape, F32),
        scratch_shapes=[pltpu.SemaphoreType.DMA(())],
        input_output_aliases={2: 0},
        compiler_params=_params(("arbitrary",)),
        name="dispatch",
    )(pos3, hext, init)


def _experts_kernel(elo_ref, ehi_ref, valid_ref, x_ref, wgu_lo_ref, wgu_hi_ref, wd_lo_ref, wd_hi_ref, y_ref):
    j = pl.program_id(0)

    @pl.when(valid_ref[j] > 0)
    def _():
        xb = x_ref[:, :D_MODEL].astype(BF16)
        wts = x_ref[:, D_MODEL:]

        def expert(wgu_ref, wd_ref):
            gu = _dot(xb, wgu_ref[0])
            gate = gu[:, :D_EXPERT]
            act = gate * _sigmoid(gate) * gu[:, D_EXPERT:]
            return _dot(act.astype(BF16), wd_ref[0])

        y_ref[...] = (wts[:, 0:1] * expert(wgu_lo_ref, wd_lo_ref)
                      + wts[:, 1:2] * expert(wgu_hi_ref, wd_hi_ref))

    @pl.when(valid_ref[j] == 0)
    def _():
        y_ref[...] = jnp.zeros(y_ref.shape, F32)


def _experts(tile_lo, tile_hi, tile_valid, hs, wgu, wd, *, tm):
    p, width = hs.shape
    grid_spec = pltpu.PrefetchScalarGridSpec(
        num_scalar_prefetch=3,
        grid=(p // tm,),
        in_specs=[pl.BlockSpec((tm, width), lambda j, lo, hi, v: (j, 0)),
                  pl.BlockSpec((1, D_MODEL, 2 * D_EXPERT), lambda j, lo, hi, v: (lo[j], 0, 0)),
                  pl.BlockSpec((1, D_MODEL, 2 * D_EXPERT), lambda j, lo, hi, v: (hi[j], 0, 0)),
                  pl.BlockSpec((1, D_EXPERT, D_MODEL), lambda j, lo, hi, v: (lo[j], 0, 0)),
                  pl.BlockSpec((1, D_EXPERT, D_MODEL), lambda j, lo, hi, v: (hi[j], 0, 0))],
        out_specs=pl.BlockSpec((tm, D_MODEL), lambda j, lo, hi, v: (j, 0)),
    )
    return pl.pallas_call(
        _experts_kernel,
        grid_spec=grid_spec,
        out_shape=jax.ShapeDtypeStruct((p, D_MODEL), F32),
        compiler_params=_params(("arbitrary",)),
        name="experts",
    )(tile_lo, tile_hi, tile_valid, hs, wgu, wgu, wd, wd)


def _combine_kernel(pos_ref, ys_ref, x1_ref, mod_ref, lng_ref, lnb_ref, out_ref, ybuf_ref, sem, *, rb, alpha):
    def row_copy(src_row, r):
        return pltpu.make_async_copy(ys_ref.at[pl.ds(src_row, 1)], ybuf_ref.at[pl.ds(r, 1)], sem)

    def issue(r, carry):
        row_copy(pos_ref[0, 0, r], r).start()
        return carry

    lax.fori_loop(0, rb, issue, 0)

    def drain(r, carry):
        row_copy(0, r).wait()
        return carry

    lax.fori_loop(0, rb, drain, 0)

    m = mod_ref[0]
    out_ref[...] = _layer_norm(alpha * x1_ref[...] + m[5:6] * ybuf_ref[...], lng_ref[...], lnb_ref[...])


def _combine(pos3, ys, x1, mod, lng, lnb, *, rb, n_rows, n_lat, seq, batch, alpha):
    nlat = n_lat // rb

    def mod_idx(i):
        return (jnp.where(i < nlat, (i * rb) // seq, batch), 0, 0)

    const = lambda i: (0, 0)
    kern = functools.partial(_combine_kernel, rb=rb, alpha=alpha)
    return pl.pallas_call(
        kern,
        grid=(n_rows // rb,),
        in_specs=[pl.BlockSpec((1, 1, rb), lambda i: (i, 0, 0), memory_space=pltpu.SMEM),
                  pl.BlockSpec(memory_space=pl.ANY),
                  pl.BlockSpec((rb, D_MODEL), lambda i: (i, 0)),
                  pl.BlockSpec((1, N_MOD, D_MODEL), mod_idx),
                  pl.BlockSpec((1, D_MODEL), const),
                  pl.BlockSpec((1, D_MODEL), const)],
        out_specs=pl.BlockSpec((rb, D_MODEL), lambda i: (i, 0)),
        out_shape=jax.ShapeDtypeStruct((n_rows, D_MODEL), F32),
        scratch_shapes=[pltpu.VMEM((rb, D_MODEL), F32), pltpu.SemaphoreType.DMA(())],
        compiler_params=_params(("arbitrary",)),
        name="combine",
    )(pos3, ys, x1, mod, lng, lnb)


def _rope_tables(seq, pad_rows):
    rows = seq // GRID_W
    row = jnp.repeat(jnp.arange(rows, dtype=F32), GRID_W)
    col = jnp.tile(jnp.arange(GRID_W, dtype=F32), rows)
    inv_freq = ROPE_BASE ** (-jnp.arange(0, ROPE_AXIS_DIM, 2, dtype=F32) / ROPE_AXIS_DIM)
    ang = jnp.concatenate([row[:, None] * inv_freq, col[:, None] * inv_freq], -1)
    cos, sin = jnp.cos(ang), jnp.sin(ang)
    cos64 = jnp.concatenate([cos, cos], -1)
    sin64 = jnp.concatenate([-sin, sin], -1)
    cos_t = jnp.concatenate([cos64, cos64], -1)
    sin_t = jnp.concatenate([sin64, sin64], -1)
    cos_t = jnp.concatenate([cos_t, jnp.ones((pad_rows, LANES), F32)], 0)
    sin_t = jnp.concatenate([sin_t, jnp.zeros((pad_rows, LANES), F32)], 0)
    return cos_t, sin_t


def _class_tables():
    lo, hi = [], []
    for g in range(N_GROUPS):
        for a in range(EXPERTS_PER_GROUP):
            for b in range(a + 1, EXPERTS_PER_GROUP):
                lo.append(g * EXPERTS_PER_GROUP + a)
                hi.append(g * EXPERTS_PER_GROUP + b)
    return jnp.array(lo, jnp.int32), jnp.array(hi, jnp.int32)


def kernel(x, c, ctx, c_ctx, w_ada, b_ada, w_in, lam_q1, lam_k1, lam_q2, lam_k2, subln_g, w_attn_o, conv_w, conv_b, conv_ln_g, conv_ln_b, w_conv_o, w_out, ln1_g, ln1_b, w_router, b_router, w_e_gate, w_e_up, w_e_down, ln2_g, ln2_b):
    batch, seq, d = x.shape
    n_ctx = ctx.shape[1]
    depth = w_ada.shape[0]
    n_lat = batch * seq
    t = n_lat + batch * n_ctx
    alpha = (2 * depth) ** 0.25

    rb = 512 if (seq % 512 == 0 and (batch * n_ctx) % 512 == 0) else 256
    rbd = 256
    tm = 256
    qb = 512 if seq % 512 == 0 else 256
    kc = 512 if seq % 512 == 0 else 256
    assert batch + 1 <= 8 and seq % rb == 0 and seq % n_ctx == 0 and n_ctx % CONV_HALO == 0

    q_end, k_end, v_end = QK_W, 2 * QK_W, 3 * QK_W
    u_end = v_end + 2 * CONV_C
    perm64 = jnp.concatenate([jnp.arange(0, HEAD_DIM, 2), jnp.arange(1, HEAD_DIM, 2)])
    perm_qk = (jnp.arange(2 * QK_W // HEAD_DIM)[:, None] * HEAD_DIM + perm64[None, :]).reshape(-1)

    cos_t, sin_t = _rope_tables(seq, rb)
    cls_lo, cls_hi = _class_tables()
    p_rows = ((t + tm - 1) // tm + N_CLASSES) * tm

    cs = jnp.zeros((8, d), F32).at[:batch].set(c).at[batch].set(c_ctx)
    x_all = jnp.concatenate([x.reshape(n_lat, d), ctx.reshape(batch * n_ctx, d)], 0)
    wr_hi, wr_lo = _split_bf16(w_router.T)
    br = b_router.reshape(N_EXPERTS, 1)

    for l in range(depth):
        last = l == depth - 1
        lam_init = 0.8 - 0.6 * math.exp(-0.3 * l)
        n_rows = n_lat if last else t

        mod = _ada(cs, w_ada[l], b_ada[l].reshape(1, -1)).reshape(8, N_MOD, d)

        w_l = w_in[l]
        wqk = w_l[:, :k_end][:, perm_qk].astype(BF16)
        wvt = w_l[:, k_end:v_end].T.astype(BF16)
        wu = w_l[:, v_end:u_end].astype(BF16)
        wg = w_l[:, u_end:].astype(BF16)
        q, k1, k2, vt, y, gates = _inproj(x_all, mod, cos_t, sin_t, wqk, wvt, wu, wg,
                                          rb=rb, n_lat=n_lat, seq=seq, batch=batch)

        lamv = jnp.stack([lam_q1[l], lam_k1[l], lam_q2[l], lam_k2[l]])
        g_col = subln_g[l].reshape(HEAD_W, 1)
        o = _attn_latent(q, k1, k2, vt, lamv, g_col, lam_init=lam_init, batch=batch, seq=seq,
                         ctx=n_ctx, qb=qb, kc=kc)
        if not last:
            o = _attn_ctx(o, q, k1, k2, vt, lamv, g_col, lam_init=lam_init, batch=batch, seq=seq, ctx=n_ctx)

        cw = jnp.concatenate([conv_w[l], jnp.zeros((1, CONV_C), F32)], 0)
        z = _conv(y, cw, conv_b[l].reshape(1, -1), conv_ln_g[l].reshape(1, -1), conv_ln_b[l].reshape(1, -1),
                  n_rows=n_rows, n_lat=n_lat, seq=seq, ctx=n_ctx)

        x1, hext, meta, counts = _merge(
            o, z, gates, x_all, mod, w_attn_o[l].astype(BF16), w_conv_o[l].astype(BF16), w_out[l].astype(BF16),
            ln1_g[l].reshape(1, -1), ln1_b[l].reshape(1, -1), wr_hi, wr_lo, br,
            rb=rb, n_rows=n_rows, n_lat=n_lat, seq=seq, batch=batch, alpha=alpha)

        cnt = counts[:N_CLASSES, 0].astype(jnp.int32)
        padded = ((cnt + tm - 1) // tm) * tm
        ends = jnp.cumsum(padded)
        starts = ends - padded
        cls = meta[0, :n_rows].astype(jnp.int32)
        pos = starts[cls] + meta[1, :n_rows].astype(jnp.int32)
        pos3 = pos.reshape(n_rows // rbd, 1, rbd)
        tile_row0 = jnp.arange(p_rows // tm, dtype=jnp.int32) * tm
        tile_cls = jnp.minimum(jnp.sum(tile_row0[:, None] >= ends[None, :], axis=1), N_CLASSES - 1)
        tile_valid = (tile_row0 < ends[-1]).astype(jnp.int32)
        tile_lo, tile_hi = cls_lo[tile_cls], cls_hi[tile_cls]

        hs = _dispatch(pos3, hext, jnp.zeros((p_rows, d + META_W), F32), rb=rbd, n_rows=n_rows)
        wgu = jnp.concatenate([w_e_gate[l], w_e_up[l]], -1).astype(BF16)
        ys = _experts(tile_lo, tile_hi, tile_valid, hs, wgu, w_e_down[l].astype(BF16), tm=tm)
        x_all = _combine(pos3, ys, x1, mod, ln2_g[l].reshape(1, -1), ln2_b[l].reshape(1, -1),
                         rb=rbd, n_rows=n_rows, n_lat=n_lat, seq=seq, batch=batch, alpha=alpha)

    return x_all.reshape(batch, seq, d)
```

```python
import functools
import math

import jax
import jax.numpy as jnp
from jax import lax
from jax.experimental import pallas as pl
from jax.experimental.pallas import tpu as pltpu

F32 = jnp.float32
BF16 = jnp.bfloat16

D_MODEL = 1024
HEADS = 4
HEAD_DIM = 64
HEAD_W = 2 * HEAD_DIM
QK_W = HEADS * HEAD_W
CONV_C = 512
CONV_K = 31
CONV_HALO = 16
N_EXPERTS = 16
N_GROUPS = 4
EXPERTS_PER_GROUP = 4
PAIRS_PER_GROUP = 6
N_CLASSES = N_GROUPS * PAIRS_PER_GROUP
CLASS_ROWS = 32
D_EXPERT = 512
N_MOD = 6
GRID_W = 64
ROPE_AXIS_DIM = HEAD_DIM // 2
ROPE_BASE = 10000.0
EPS = 1e-5
LANES = 128
SUBLANES = 8
META_W = LANES
NEG_BIG = -1e30

VMEM_LIMIT = 56 * 1024 * 1024


def _dot(a, b):
    return jnp.dot(a, b, preferred_element_type=F32)


def _dot_nt(a, b):
    return lax.dot_general(a, b, (((1,), (1,)), ((), ())), preferred_element_type=F32)


def _split_bf16(a):
    hi = a.astype(BF16)
    lo = (a - hi.astype(F32)).astype(BF16)
    return hi, lo


def _sigmoid(x):
    return 1.0 / (1.0 + jnp.exp(-x))


def _layer_norm(x, g, b):
    mu = jnp.mean(x, axis=-1, keepdims=True)
    xc = x - mu
    var = jnp.mean(xc * xc, axis=-1, keepdims=True)
    return xc * lax.rsqrt(var + EPS) * g + b


def _params(sem):
    return pltpu.CompilerParams(dimension_semantics=sem, vmem_limit_bytes=VMEM_LIMIT)


def _ada_kernel(c_ref, w_ref, b_ref, o_ref):
    c = c_ref[...]
    a = c * _sigmoid(c)
    a_hi, a_lo = _split_bf16(a)
    w_hi, w_lo = _split_bf16(w_ref[...])
    o_ref[...] = _dot(a_hi, w_hi) + _dot(a_lo, w_hi) + _dot(a_hi, w_lo) + b_ref[...]


def _ada(cs, w, b):
    n = w.shape[1]
    nb = 1536
    return pl.pallas_call(
        _ada_kernel,
        grid=(n // nb,),
        in_specs=[pl.BlockSpec((8, D_MODEL), lambda j: (0, 0)),
                  pl.BlockSpec((D_MODEL, nb), lambda j: (0, j)),
                  pl.BlockSpec((1, nb), lambda j: (0, j))],
        out_specs=pl.BlockSpec((8, nb), lambda j: (0, j)),
        out_shape=jax.ShapeDtypeStruct((8, n), F32),
        compiler_params=_params(("arbitrary",)),
        name="ada",
    )(cs, w, b)


def _inproj_kernel(x_ref, mod_ref, cos_ref, sin_ref, wqk_ref, wvt_ref, wu_ref, wg_ref,
                   q_ref, k1_ref, k2_ref, vt_ref, y_ref, g_ref):
    m = mod_ref[0]
    hx = (x_ref[...] * (1.0 + m[1:2]) + m[0:1]).astype(BF16)

    qk = _dot(hx, wqk_ref[...])
    w = qk.shape[1]
    lane = lax.broadcasted_iota(jnp.int32, qk.shape, 1)
    first_half = (lane % HEAD_DIM) < (HEAD_DIM // 2)
    partner = jnp.where(first_half,
                        pltpu.roll(qk, w - HEAD_DIM // 2, 1),
                        pltpu.roll(qk, HEAD_DIM // 2, 1))
    reps = w // LANES
    cos = jnp.concatenate([cos_ref[...]] * reps, axis=1)
    sin = jnp.concatenate([sin_ref[...]] * reps, axis=1)
    roped = qk * cos + partner * sin
    q_ref[...] = (roped[:, :QK_W] * (HEAD_DIM ** -0.5 * LOG2E)).astype(BF16)
    k = roped[:, QK_W:]
    map1 = (lax.broadcasted_iota(jnp.int32, k.shape, 1) % HEAD_W) < HEAD_DIM
    k1_ref[...] = jnp.where(map1, k, 0.0).astype(BF16)
    k2_ref[...] = jnp.where(map1, 0.0, k).astype(BF16)

    vt_ref[...] = _dot_nt(wvt_ref[...], hx).astype(BF16)

    u = _dot(hx, wu_ref[...])
    y_ref[...] = u[:, :CONV_C] * _sigmoid(u[:, CONV_C:])
    g_ref[...] = _sigmoid(_dot(hx, wg_ref[...])).astype(BF16)


def _inproj(x_all, mod, cos_t, sin_t, wqk, wvt, wu, wg, *, rb, n_lat, seq, batch):
    t = x_all.shape[0]
    nlat = n_lat // rb
    per_seq = seq // rb

    def mod_idx(i):
        return (jnp.where(i < nlat, (i * rb) // seq, batch), 0, 0)

    def rope_idx(i):
        return (jnp.where(i < nlat, i % per_seq, per_seq), 0)

    const = lambda i: (0, 0)
    row = lambda i: (i, 0)
    return pl.pallas_call(
        _inproj_kernel,
        grid=(t // rb,),
        in_specs=[pl.BlockSpec((rb, D_MODEL), row),
                  pl.BlockSpec((1, N_MOD, D_MODEL), mod_idx),
                  pl.BlockSpec((rb, LANES), rope_idx),
                  pl.BlockSpec((rb, LANES), rope_idx),
                  pl.BlockSpec(wqk.shape, const),
                  pl.BlockSpec(wvt.shape, const),
                  pl.BlockSpec(wu.shape, const),
                  pl.BlockSpec(wg.shape, const)],
        out_specs=[pl.BlockSpec((rb, QK_W), row),
                   pl.BlockSpec((rb, QK_W), row),
                   pl.BlockSpec((rb, QK_W), row),
                   pl.BlockSpec((QK_W, rb), lambda i: (0, i)),
                   pl.BlockSpec((rb, CONV_C), row),
                   pl.BlockSpec((rb, 2 * D_MODEL), row)],
        out_shape=[jax.ShapeDtypeStruct((t, QK_W), BF16),
                   jax.ShapeDtypeStruct((t, QK_W), BF16),
                   jax.ShapeDtypeStruct((t, QK_W), BF16),
                   jax.ShapeDtypeStruct((QK_W, t), BF16),
                   jax.ShapeDtypeStruct((t, CONV_C), F32),
                   jax.ShapeDtypeStruct((t, 2 * D_MODEL), BF16)],
        compiler_params=_params(("arbitrary",)),
        name="inproj",
    )(x_all, mod, cos_t, sin_t, wqk, wvt, wu, wg)


ONES_ROWS = 16
ACC_ROWS = HEAD_W + ONES_ROWS
LOG2E = math.log2(math.e)


def _attn_kernel(*refs, lam_init, n_chunks, kc):
    if n_chunks:
        (q_ref, kc1_ref, kc2_ref, vtc_ref, k1_ref, k2_ref, vt_ref, lam_ref, g_ref,
         o_ref, acc1_ref, acc2_ref, sa1_ref, sa2_ref, sb1_ref, sb2_ref) = refs
        key_refs = (k1_ref, k2_ref)
    else:
        (q_ref, kc1_ref, kc2_ref, vtc_ref, lam_ref, g_ref, o_ref, acc1_ref, acc2_ref) = refs
    acc_refs = (acc1_ref, acc2_ref)
    q = q_ref[...]
    qb = q.shape[0]

    def with_ones(vt):
        return jnp.concatenate([vt, jnp.ones((ONES_ROWS, vt.shape[1]), BF16)], axis=0)

    def update(i, s, cmax, m_old, vt_ext, first=False):
        m_new = jnp.maximum(m_old, cmax)
        p = jnp.exp2(s - m_new).astype(BF16)
        pv = _dot(vt_ext, p)
        if first:
            acc_refs[i][...] = pv
        else:
            acc_refs[i][...] = jnp.exp2(m_old - m_new) * acc_refs[i][...] + pv
        return m_new

    vtc_ext = with_ones(vtc_ref[...])
    ms = []
    for i, kref in enumerate((kc1_ref, kc2_ref)):
        s = _dot_nt(kref[...], q)
        ms.append(update(i, s, jnp.max(s, axis=0, keepdims=True), jnp.full((1, qb), NEG_BIG, F32), vtc_ext,
                         first=True))

    if n_chunks:
        def scores(c, s_refs):
            off = pl.multiple_of(c * kc, kc)
            cms = []
            for i in range(2):
                s = _dot_nt(key_refs[i][pl.ds(off, kc), :], q)
                s_refs[i][...] = s
                cms.append(jnp.max(s, axis=0, keepdims=True))
            return tuple(cms)

        def consume(c, s_refs, cms, m):
            off = pl.multiple_of(c * kc, kc)
            vt_ext = with_ones(vt_ref[:, pl.ds(off, kc)])
            return tuple(update(i, s_refs[i][...], cms[i], m[i], vt_ext) for i in range(2))

        sa = (sa1_ref, sa2_ref)
        sb = (sb1_ref, sb2_ref)
        cm_a = scores(0, sa)

        def body(j, carry):
            m = carry[:2]
            cm_a = carry[2:]
            c = 2 * j
            cm_b = scores(c + 1, sb)
            m = consume(c, sa, cm_a, m)
            cm_a = scores(c + 2, sa)
            m = consume(c + 1, sb, cm_b, m)
            return (*m, *cm_a)

        carry = lax.fori_loop(0, n_chunks // 2 - 1, body, (*ms, *cm_a))
        m, cm_a = carry[:2], carry[2:]
        cm_b = scores(n_chunks - 1, sb)
        m = consume(n_chunks - 2, sa, cm_a, m)
        consume(n_chunks - 1, sb, cm_b, m)

    lv = lam_ref[...]
    lam = (jnp.exp(jnp.sum(lv[0:1] * lv[1:2], axis=1, keepdims=True))
           - jnp.exp(jnp.sum(lv[2:3] * lv[3:4], axis=1, keepdims=True)) + lam_init)
    o1 = acc1_ref[0:HEAD_W, :] / acc1_ref[HEAD_W:HEAD_W + 1, :]
    o2 = acc2_ref[0:HEAD_W, :] / acc2_ref[HEAD_W:HEAD_W + 1, :]
    o_t = o1 - lam * o2
    mean_sq = jnp.mean(o_t * o_t, axis=0, keepdims=True)
    o_t = o_t * lax.rsqrt(mean_sq + EPS) * (g_ref[...] * (1.0 - lam_init))
    o_ref[...] = o_t.T.astype(BF16)


def _attn_latent(q, k1, k2, vt, lamv, g_col, *, lam_init, batch, seq, ctx, qb, kc):
    n_lat = batch * seq
    nq = seq // qb
    ctx0 = n_lat // ctx
    n_chunks = seq // kc
    assert n_chunks >= 2 and n_chunks % 2 == 0
    kern = functools.partial(_attn_kernel, lam_init=lam_init, n_chunks=n_chunks, kc=kc)
    return pl.pallas_call(
        kern,
        grid=(batch, HEADS, nq),
        in_specs=[pl.BlockSpec((qb, HEAD_W), lambda b, h, i: (b * nq + i, h)),
                  pl.BlockSpec((ctx, HEAD_W), lambda b, h, i: (ctx0 + b, h)),
                  pl.BlockSpec((ctx, HEAD_W), lambda b, h, i: (ctx0 + b, h)),
                  pl.BlockSpec((HEAD_W, ctx), lambda b, h, i: (h, ctx0 + b)),
                  pl.BlockSpec((seq, HEAD_W), lambda b, h, i: (b, h)),
                  pl.BlockSpec((seq, HEAD_W), lambda b, h, i: (b, h)),
                  pl.BlockSpec((HEAD_W, seq), lambda b, h, i: (h, b)),
                  pl.BlockSpec((4, HEAD_DIM), lambda b, h, i: (0, 0)),
                  pl.BlockSpec((HEAD_W, 1), lambda b, h, i: (0, 0))],
        out_specs=pl.BlockSpec((qb, HEAD_W), lambda b, h, i: (b * nq + i, h)),
        out_shape=jax.ShapeDtypeStruct((n_lat, QK_W), BF16),
        scratch_shapes=[pltpu.VMEM((ACC_ROWS, qb), F32)] * 2 + [pltpu.VMEM((kc, qb), F32)] * 4,
        compiler_params=_params(("arbitrary", "arbitrary", "arbitrary")),
        name="attn_latent",
    )(q, k1, k2, vt, k1, k2, vt, lamv, g_col)


def _attn_ctx(q, k1, k2, vt, lamv, g_col, *, lam_init, batch, seq, ctx):
    ctx0 = batch * seq // ctx
    kern = functools.partial(_attn_kernel, lam_init=lam_init, n_chunks=0, kc=0)
    return pl.pallas_call(
        kern,
        grid=(batch, HEADS),
        in_specs=[pl.BlockSpec((ctx, HEAD_W), lambda b, h: (ctx0 + b, h)),
                  pl.BlockSpec((ctx, HEAD_W), lambda b, h: (ctx0 + b, h)),
                  pl.BlockSpec((ctx, HEAD_W), lambda b, h: (ctx0 + b, h)),
                  pl.BlockSpec((HEAD_W, ctx), lambda b, h: (h, ctx0 + b)),
                  pl.BlockSpec((4, HEAD_DIM), lambda b, h: (0, 0)),
                  pl.BlockSpec((HEAD_W, 1), lambda b, h: (0, 0))],
        out_specs=pl.BlockSpec((ctx, HEAD_W), lambda b, h: (b, h)),
        out_shape=jax.ShapeDtypeStruct((batch * ctx, QK_W), BF16),
        scratch_shapes=[pltpu.VMEM((ACC_ROWS, ctx), F32)] * 2,
        compiler_params=_params(("arbitrary", "arbitrary")),
        name="attn_ctx",
    )(q, k1, k2, vt, lamv, g_col)


CONV_ROW_TILE = 32


def _conv_kernel(y_ref, prev_ref, next_ref, w_ref, b_ref, g_ref, beta_ref, z_ref, buf_ref, shifted_ref, *,
                 rb, nlat, per_seq):
    i = pl.program_id(0)
    pos = i % per_seq
    first = jnp.logical_or(i >= nlat, pos == 0)
    last = jnp.logical_or(i >= nlat, pos == per_seq - 1)

    buf_ref[CONV_HALO:CONV_HALO + rb, :] = y_ref[...]

    @pl.when(first)
    def _():
        buf_ref[0:CONV_HALO, :] = jnp.zeros((CONV_HALO, CONV_C), F32)

    @pl.when(jnp.logical_not(first))
    def _():
        buf_ref[0:CONV_HALO, :] = prev_ref[...]

    @pl.when(last)
    def _():
        buf_ref[CONV_HALO + rb:, :] = jnp.zeros((CONV_HALO, CONV_C), F32)

    @pl.when(jnp.logical_not(last))
    def _():
        buf_ref[CONV_HALO + rb:, :] = next_ref[...]

    n_sh = shifted_ref.shape[1]
    for r in range(1, SUBLANES):
        shifted_ref[r] = buf_ref[r:r + n_sh, :]

    base = CONV_HALO - CONV_K // 2
    for r0 in range(0, rb, CONV_ROW_TILE):
        acc = jnp.zeros((CONV_ROW_TILE, CONV_C), F32) + b_ref[...]
        for k in range(CONV_K):
            a, r = divmod(base + k, SUBLANES)
            lo = r0 + a * SUBLANES
            rows = buf_ref[lo:lo + CONV_ROW_TILE, :] if r == 0 else shifted_ref[r, lo:lo + CONV_ROW_TILE, :]
            acc = acc + w_ref[k:k + 1, :] * rows
        v = _layer_norm(acc, g_ref[...], beta_ref[...])
        z_ref[r0:r0 + CONV_ROW_TILE, :] = (v * _sigmoid(v)).astype(BF16)


def _conv(y, w, b, g, beta, *, n_rows, n_lat, seq, ctx):
    rb = ctx
    hb = rb // CONV_HALO
    t = y.shape[0]
    n_halo = t // CONV_HALO
    kern = functools.partial(_conv_kernel, rb=rb, nlat=n_lat // rb, per_seq=seq // rb)
    const = lambda i: (0, 0)
    return pl.pallas_call(
        kern,
        grid=(n_rows // rb,),
        in_specs=[pl.BlockSpec((rb, CONV_C), lambda i: (i, 0)),
                  pl.BlockSpec((CONV_HALO, CONV_C), lambda i: (jnp.maximum(i * hb - 1, 0), 0)),
                  pl.BlockSpec((CONV_HALO, CONV_C), lambda i: (jnp.minimum((i + 1) * hb, n_halo - 1), 0)),
                  pl.BlockSpec(w.shape, const),
                  pl.BlockSpec((1, CONV_C), const),
                  pl.BlockSpec((1, CONV_C), const),
                  pl.BlockSpec((1, CONV_C), const)],
        out_specs=pl.BlockSpec((rb, CONV_C), lambda i: (i, 0)),
        out_shape=jax.ShapeDtypeStruct((n_rows, CONV_C), BF16),
        scratch_shapes=[pltpu.VMEM((rb + 2 * CONV_HALO, CONV_C), F32),
                        pltpu.VMEM((SUBLANES, rb + 2 * CONV_HALO - SUBLANES, CONV_C), F32)],
        compiler_params=_params(("arbitrary",)),
        name="conv",
    )(y, y, y, w, b, g, beta)


def _first_index(vals, target):
    idx = jnp.full(target.shape, len(vals) - 1, jnp.int32)
    for j in range(len(vals) - 2, -1, -1):
        idx = jnp.where(vals[j] == target, j, idx)
    return idx


def _route(logits_t):
    mx = jnp.max(logits_t, axis=0, keepdims=True)
    e = jnp.exp(logits_t - mx)
    m1s, m2s, i1s, i2s, scores = [], [], [], [], []
    for g in range(N_GROUPS):
        a = [e[g * EXPERTS_PER_GROUP + j:g * EXPERTS_PER_GROUP + j + 1, :] for j in range(EXPERTS_PER_GROUP)]
        m1 = functools.reduce(jnp.maximum, a)
        i1 = _first_index(a, m1)
        rest = [jnp.where(i1 == j, -1.0, a[j]) for j in range(EXPERTS_PER_GROUP)]
        m2 = functools.reduce(jnp.maximum, rest)
        i2 = _first_index(rest, m2)
        m1s.append(m1); m2s.append(m2); i1s.append(i1); i2s.append(i2); scores.append(m1 + m2)
    gstar = _first_index(scores, functools.reduce(jnp.maximum, scores))

    def pick(xs):
        out = xs[N_GROUPS - 1]
        for g in range(N_GROUPS - 2, -1, -1):
            out = jnp.where(gstar == g, xs[g], out)
        return out

    m1, m2, i1, i2 = pick(m1s), pick(m2s), pick(i1s), pick(i2s)
    tot = m1 + m2
    w1, w2 = m1 / tot, m2 / tot
    lo = jnp.minimum(i1, i2)
    hi = jnp.maximum(i1, i2)
    w_lo = jnp.where(i1 < i2, w1, w2)
    w_hi = jnp.where(i1 < i2, w2, w1)
    pair = jnp.where(lo == 0, 0, jnp.where(lo == 1, 3, 5)) + (hi - lo - 1)
    return gstar * PAIRS_PER_GROUP + pair, w_lo, w_hi


def _merge_kernel(o_ref, octx_ref, z_ref, g_ref, x_ref, mod_ref, wao_ref, wco_ref, wout_ref, lng_ref, lnb_ref,
                  wr_hi_ref, wr_lo_ref, br_ref,
                  x1_ref, hext_ref, meta_ref, cnt_ref, carry_ref, *, alpha, nlat):
    i = pl.program_id(0)

    @pl.when(i == 0)
    def _():
        carry_ref[...] = jnp.zeros(carry_ref.shape, F32)

    m = mod_ref[0]
    att = _dot(jnp.where(i < nlat, o_ref[...], octx_ref[...]), wao_ref[...])
    cnv = _dot(z_ref[...], wco_ref[...])
    gates = g_ref[...].astype(F32)
    mixin = gates[:, :D_MODEL] * att + gates[:, D_MODEL:] * cnv
    mix = _dot(mixin.astype(BF16), wout_ref[...])
    x1 = _layer_norm(alpha * x_ref[...] + m[2:3] * mix, lng_ref[...], lnb_ref[...])
    x1_ref[...] = x1
    h = x1 * (1.0 + m[4:5]) + m[3:4]
    hext_ref[:, :D_MODEL] = h

    h_hi, h_lo = _split_bf16(h)
    logits_t = (_dot_nt(wr_hi_ref[...], h_hi) + _dot_nt(wr_hi_ref[...], h_lo)
                + _dot_nt(wr_lo_ref[...], h_hi) + br_ref[...])
    cls, w_lo, w_hi = _route(logits_t)
    rb = h.shape[0]

    crow = lax.broadcasted_iota(jnp.int32, (CLASS_ROWS, rb), 0)
    onehot = (crow == cls).astype(F32)
    tri = (lax.broadcasted_iota(jnp.int32, (rb, rb), 0)
           < lax.broadcasted_iota(jnp.int32, (rb, rb), 1)).astype(BF16)
    before = _dot(onehot.astype(BF16), tri) + carry_ref[:, 0:1]
    rank = jnp.sum(onehot * before, axis=0, keepdims=True)
    carry_ref[...] = carry_ref[...] + jnp.sum(onehot, axis=1, keepdims=True)
    cnt_ref[...] = carry_ref[...]

    mrow = lax.broadcasted_iota(jnp.int32, (8, rb), 0)
    meta_ref[...] = jnp.where(mrow == 0, cls.astype(F32), jnp.where(mrow == 1, rank, 0.0))

    wrow = lax.broadcasted_iota(jnp.int32, (META_W, rb), 0)
    wts_t = jnp.where(wrow == 0, w_lo, jnp.where(wrow == 1, w_hi, 0.0))
    hext_ref[:, D_MODEL:] = wts_t.T


def _merge(o, o_ctx, z, gates, x_all, mod, wao, wco, wout, lng, lnb, wr_hi, wr_lo, br, *,
           rb, n_rows, n_lat, seq, batch, alpha):
    t = n_rows
    nlat = n_lat // rb

    def mod_idx(i):
        return (jnp.where(i < nlat, (i * rb) // seq, batch), 0, 0)

    const = lambda i: (0, 0)
    row = lambda i: (i, 0)
    kern = functools.partial(_merge_kernel, alpha=alpha, nlat=nlat)
    return pl.pallas_call(
        kern,
        grid=(n_rows // rb,),
        in_specs=[pl.BlockSpec((rb, QK_W), lambda i: (jnp.minimum(i, nlat - 1), 0)),
                  pl.BlockSpec((rb, QK_W), lambda i: (jnp.maximum(i - nlat, 0), 0)),
                  pl.BlockSpec((rb, CONV_C), row),
                  pl.BlockSpec((rb, 2 * D_MODEL), row),
                  pl.BlockSpec((rb, D_MODEL), row),
                  pl.BlockSpec((1, N_MOD, D_MODEL), mod_idx),
                  pl.BlockSpec(wao.shape, const),
                  pl.BlockSpec(wco.shape, const),
                  pl.BlockSpec(wout.shape, const),
                  pl.BlockSpec((1, D_MODEL), const),
                  pl.BlockSpec((1, D_MODEL), const),
                  pl.BlockSpec(wr_hi.shape, const),
                  pl.BlockSpec(wr_lo.shape, const),
                  pl.BlockSpec((N_EXPERTS, 1), const)],
        out_specs=[pl.BlockSpec((rb, D_MODEL), row),
                   pl.BlockSpec((rb, D_MODEL + META_W), row),
                   pl.BlockSpec((8, rb), lambda i: (0, i)),
                   pl.BlockSpec((CLASS_ROWS, LANES), const)],
        out_shape=[jax.ShapeDtypeStruct((t, D_MODEL), F32),
                   jax.ShapeDtypeStruct((t, D_MODEL + META_W), F32),
                   jax.ShapeDtypeStruct((8, t), F32),
                   jax.ShapeDtypeStruct((CLASS_ROWS, LANES), F32)],
        scratch_shapes=[pltpu.VMEM((CLASS_ROWS, LANES), F32)],
        compiler_params=_params(("arbitrary",)),
        name="merge",
    )(o, o_ctx, z, gates, x_all, mod, wao, wco, wout, lng, lnb, wr_hi, wr_lo, br)


def _dispatch_kernel(pos_ref, h_ref, init_ref, out_ref, sem, *, rb, n_steps):
    del init_ref
    i = pl.program_id(0)
    slot = i % 2
    base = i * rb

    def issue(r, carry):
        pltpu.make_async_copy(h_ref.at[pl.ds(base + r, 1)], out_ref.at[pl.ds(pos_ref[0, 0, r], 1)],
                              sem.at[slot]).start()
        return carry

    lax.fori_loop(0, rb, issue, 0, unroll=8)

    def wait_block(s):
        pltpu.make_async_copy(h_ref.at[pl.ds(0, rb)], out_ref.at[pl.ds(0, rb)], sem.at[s]).wait()

    @pl.when(i > 0)
    def _():
        wait_block(1 - slot)

    @pl.when(i == n_steps - 1)
    def _():
        wait_block(slot)


def _dispatch(pos3, hext, init, *, rb, n_rows):
    n_steps = n_rows // rb
    kern = functools.partial(_dispatch_kernel, rb=rb, n_steps=n_steps)
    return pl.pallas_call(
        kern,
        grid=(n_steps,),
        in_specs=[pl.BlockSpec((1, 1, rb), lambda i: (i, 0, 0), memory_space=pltpu.SMEM),
                  pl.BlockSpec(memory_space=pl.ANY),
                  pl.BlockSpec(memory_space=pl.ANY)],
        out_specs=pl.BlockSpec(memory_space=pl.ANY),
        out_shape=jax.ShapeDtypeStruct(init.shape, F32),
        scratch_shapes=[pltpu.SemaphoreType.DMA((2,))],
        input_output_aliases={2: 0},
        compiler_params=_params(("arbitrary",)),
        name="dispatch",
    )(pos3, hext, init)


def _experts_kernel(elo_ref, ehi_ref, valid_ref, x_ref, wgu_lo_ref, wgu_hi_ref, wd_lo_ref, wd_hi_ref, y_ref):
    j = pl.program_id(0)

    @pl.when(valid_ref[j] > 0)
    def _():
        xb = x_ref[:, :D_MODEL].astype(BF16)
        wts = x_ref[:, D_MODEL:]

        def expert(wgu_ref, wd_ref):
            gu = _dot(xb, wgu_ref[0])
            gate = gu[:, :D_EXPERT]
            act = gate * _sigmoid(gate) * gu[:, D_EXPERT:]
            return _dot(act.astype(BF16), wd_ref[0])

        y_ref[...] = (wts[:, 0:1] * expert(wgu_lo_ref, wd_lo_ref)
                      + wts[:, 1:2] * expert(wgu_hi_ref, wd_hi_ref))

    @pl.when(valid_ref[j] == 0)
    def _():
        y_ref[...] = jnp.zeros(y_ref.shape, F32)


def _experts(tile_lo, tile_hi, tile_valid, hs, wgu, wd, *, tm):
    p, width = hs.shape
    grid_spec = pltpu.PrefetchScalarGridSpec(
        num_scalar_prefetch=3,
        grid=(p // tm,),
        in_specs=[pl.BlockSpec((tm, width), lambda j, lo, hi, v: (j, 0)),
                  pl.BlockSpec((1, D_MODEL, 2 * D_EXPERT), lambda j, lo, hi, v: (lo[j], 0, 0)),
                  pl.BlockSpec((1, D_MODEL, 2 * D_EXPERT), lambda j, lo, hi, v: (hi[j], 0, 0)),
                  pl.BlockSpec((1, D_EXPERT, D_MODEL), lambda j, lo, hi, v: (lo[j], 0, 0)),
                  pl.BlockSpec((1, D_EXPERT, D_MODEL), lambda j, lo, hi, v: (hi[j], 0, 0))],
        out_specs=pl.BlockSpec((tm, D_MODEL), lambda j, lo, hi, v: (j, 0)),
    )
    return pl.pallas_call(
        _experts_kernel,
        grid_spec=grid_spec,
        out_shape=jax.ShapeDtypeStruct((p, D_MODEL), F32),
        compiler_params=_params(("arbitrary",)),
        name="experts",
    )(tile_lo, tile_hi, tile_valid, hs, wgu, wgu, wd, wd)


def _combine_kernel(pos_ref, pos_next_ref, ys_ref, x1_ref, mod_ref, lng_ref, lnb_ref, out_ref, ybuf_ref, sem, *,
                    rb, alpha, n_steps):
    i = pl.program_id(0)
    slot = i % 2

    def gather(p_ref, dst_slot):
        def issue(r, carry):
            pltpu.make_async_copy(ys_ref.at[pl.ds(p_ref[0, 0, r], 1)], ybuf_ref.at[dst_slot, pl.ds(r, 1)],
                                  sem.at[dst_slot]).start()
            return carry
        lax.fori_loop(0, rb, issue, 0, unroll=8)

    @pl.when(i == 0)
    def _():
        gather(pos_ref, 0)

    @pl.when(i + 1 < n_steps)
    def _():
        gather(pos_next_ref, 1 - slot)

    pltpu.make_async_copy(ys_ref.at[pl.ds(0, rb)], ybuf_ref.at[slot], sem.at[slot]).wait()

    m = mod_ref[0]
    out_ref[...] = _layer_norm(alpha * x1_ref[...] + m[5:6] * ybuf_ref[slot], lng_ref[...], lnb_ref[...])


def _combine(pos3, ys, x1, mod, lng, lnb, *, rb, n_rows, n_lat, seq, batch, alpha):
    nlat = n_lat // rb

    def mod_idx(i):
        return (jnp.where(i < nlat, (i * rb) // seq, batch), 0, 0)

    const = lambda i: (0, 0)
    n_steps = n_rows // rb
    kern = functools.partial(_combine_kernel, rb=rb, alpha=alpha, n_steps=n_steps)
    return pl.pallas_call(
        kern,
        grid=(n_steps,),
        in_specs=[pl.BlockSpec((1, 1, rb), lambda i: (i, 0, 0), memory_space=pltpu.SMEM),
                  pl.BlockSpec((1, 1, rb), lambda i: (jnp.minimum(i + 1, n_steps - 1), 0, 0),
                               memory_space=pltpu.SMEM),
                  pl.BlockSpec(memory_space=pl.ANY),
                  pl.BlockSpec((rb, D_MODEL), lambda i: (i, 0)),
                  pl.BlockSpec((1, N_MOD, D_MODEL), mod_idx),
                  pl.BlockSpec((1, D_MODEL), const),
                  pl.BlockSpec((1, D_MODEL), const)],
        out_specs=pl.BlockSpec((rb, D_MODEL), lambda i: (i, 0)),
        out_shape=jax.ShapeDtypeStruct((n_rows, D_MODEL), F32),
        scratch_shapes=[pltpu.VMEM((2, rb, D_MODEL), F32), pltpu.SemaphoreType.DMA((2,))],
        compiler_params=_params(("arbitrary",)),
        name="combine",
    )(pos3, pos3, ys, x1, mod, lng, lnb)


def _rope_tables(seq, pad_rows):
    rows = seq // GRID_W
    row = jnp.repeat(jnp.arange(rows, dtype=F32), GRID_W)
    col = jnp.tile(jnp.arange(GRID_W, dtype=F32), rows)
    inv_freq = ROPE_BASE ** (-jnp.arange(0, ROPE_AXIS_DIM, 2, dtype=F32) / ROPE_AXIS_DIM)
    ang = jnp.concatenate([row[:, None] * inv_freq, col[:, None] * inv_freq], -1)
    cos, sin = jnp.cos(ang), jnp.sin(ang)
    cos64 = jnp.concatenate([cos, cos], -1)
    sin64 = jnp.concatenate([-sin, sin], -1)
    cos_t = jnp.concatenate([cos64, cos64], -1)
    sin_t = jnp.concatenate([sin64, sin64], -1)
    cos_t = jnp.concatenate([cos_t, jnp.ones((pad_rows, LANES), F32)], 0)
    sin_t = jnp.concatenate([sin_t, jnp.zeros((pad_rows, LANES), F32)], 0)
    return cos_t, sin_t


def _class_tables():
    lo, hi = [], []
    for g in range(N_GROUPS):
        for a in range(EXPERTS_PER_GROUP):
            for b in range(a + 1, EXPERTS_PER_GROUP):
                lo.append(g * EXPERTS_PER_GROUP + a)
                hi.append(g * EXPERTS_PER_GROUP + b)
    return jnp.array(lo, jnp.int32), jnp.array(hi, jnp.int32)


def kernel(x, c, ctx, c_ctx, w_ada, b_ada, w_in, lam_q1, lam_k1, lam_q2, lam_k2, subln_g, w_attn_o, conv_w, conv_b, conv_ln_g, conv_ln_b, w_conv_o, w_out, ln1_g, ln1_b, w_router, b_router, w_e_gate, w_e_up, w_e_down, ln2_g, ln2_b):
    batch, seq, d = x.shape
    n_ctx = ctx.shape[1]
    depth = w_ada.shape[0]
    n_lat = batch * seq
    t = n_lat + batch * n_ctx
    alpha = (2 * depth) ** 0.25

    rb = 512 if (seq % 512 == 0 and (batch * n_ctx) % 512 == 0) else 256
    rbd = 256
    tm = 256
    qb = 512 if seq % 512 == 0 else 256
    kc = 512 if seq % 512 == 0 else 256
    assert batch + 1 <= 8 and seq % rb == 0 and seq % n_ctx == 0 and n_ctx % CONV_HALO == 0

    q_end, k_end, v_end = QK_W, 2 * QK_W, 3 * QK_W
    u_end = v_end + 2 * CONV_C
    perm64 = jnp.concatenate([jnp.arange(0, HEAD_DIM, 2), jnp.arange(1, HEAD_DIM, 2)])
    perm_qk = (jnp.arange(2 * QK_W // HEAD_DIM)[:, None] * HEAD_DIM + perm64[None, :]).reshape(-1)

    cos_t, sin_t = _rope_tables(seq, rb)
    cls_lo, cls_hi = _class_tables()
    p_rows = ((t + tm - 1) // tm + N_CLASSES) * tm

    cs = jnp.zeros((8, d), F32).at[:batch].set(c).at[batch].set(c_ctx)
    x_all = jnp.concatenate([x.reshape(n_lat, d), ctx.reshape(batch * n_ctx, d)], 0)
    wr_hi, wr_lo = _split_bf16(w_router.T)
    br = b_router.reshape(N_EXPERTS, 1)

    for l in range(depth):
        last = l == depth - 1
        lam_init = 0.8 - 0.6 * math.exp(-0.3 * l)
        n_rows = n_lat if last else t

        mod = _ada(cs, w_ada[l], b_ada[l].reshape(1, -1)).reshape(8, N_MOD, d)

        w_l = w_in[l]
        wqk = w_l[:, :k_end][:, perm_qk].astype(BF16)
        wvt = w_l[:, k_end:v_end].T.astype(BF16)
        wu = w_l[:, v_end:u_end].astype(BF16)
        wg = w_l[:, u_end:].astype(BF16)
        q, k1, k2, vt, y, gates = _inproj(x_all, mod, cos_t, sin_t, wqk, wvt, wu, wg,
                                          rb=rb, n_lat=n_lat, seq=seq, batch=batch)

        lamv = jnp.stack([lam_q1[l], lam_k1[l], lam_q2[l], lam_k2[l]])
        g_col = subln_g[l].reshape(HEAD_W, 1)
        o = _attn_latent(q, k1, k2, vt, lamv, g_col, lam_init=lam_init, batch=batch, seq=seq,
                         ctx=n_ctx, qb=qb, kc=kc)
        o_ctx = o if last else _attn_ctx(q, k1, k2, vt, lamv, g_col, lam_init=lam_init, batch=batch, seq=seq,
                                         ctx=n_ctx)

        cw = jnp.concatenate([conv_w[l], jnp.zeros((1, CONV_C), F32)], 0)
        z = _conv(y, cw, conv_b[l].reshape(1, -1), conv_ln_g[l].reshape(1, -1), conv_ln_b[l].reshape(1, -1),
                  n_rows=n_rows, n_lat=n_lat, seq=seq, ctx=n_ctx)

        x1, hext, meta, counts = _merge(
            o, o_ctx, z, gates, x_all, mod, w_attn_o[l].astype(BF16), w_conv_o[l].astype(BF16), w_out[l].astype(BF16),
            ln1_g[l].reshape(1, -1), ln1_b[l].reshape(1, -1), wr_hi, wr_lo, br,
            rb=rb, n_rows=n_rows, n_lat=n_lat, seq=seq, batch=batch, alpha=alpha)

        cnt = counts[:N_CLASSES, 0].astype(jnp.int32)
        padded = ((cnt + tm - 1) // tm) * tm
        ends = jnp.cumsum(padded)
        starts = ends - padded
        cls = meta[0, :n_rows].astype(jnp.int32)
        pos = starts[cls] + meta[1, :n_rows].astype(jnp.int32)
        pos3 = pos.reshape(n_rows // rbd, 1, rbd)
        tile_row0 = jnp.arange(p_rows // tm, dtype=jnp.int32) * tm
        tile_cls = jnp.minimum(jnp.sum(tile_row0[:, None] >= ends[None, :], axis=1), N_CLASSES - 1)
        tile_valid = (tile_row0 < ends[-1]).astype(jnp.int32)
        tile_lo, tile_hi = cls_lo[tile_cls], cls_hi[tile_cls]

        hs = _dispatch(pos3, hext, jnp.zeros((p_rows, d + META_W), F32), rb=rbd, n_rows=n_rows)
        wgu = jnp.concatenate([w_e_gate[l], w_e_up[l]], -1).astype(BF16)
        ys = _experts(tile_lo, tile_hi, tile_valid, hs, wgu, w_e_down[l].astype(BF16), tm=tm)
        x_all = _combine(pos3, ys, x1, mod, ln2_g[l].reshape(1, -1), ln2_b[l].reshape(1, -1),
                         rb=rbd, n_rows=n_rows, n_lat=n_lat, seq=seq, batch=batch, alpha=alpha)

    return x_all.reshape(batch, seq, d)
```

```python
import functools
import math

import jax
import jax.numpy as jnp
from jax import lax
from jax.experimental import pallas as pl
from jax.experimental.pallas import tpu as pltpu

F32 = jnp.float32
BF16 = jnp.bfloat16

D_MODEL = 1024
HEADS = 4
HEAD_DIM = 64
HEAD_W = 2 * HEAD_DIM
QK_W = HEADS * HEAD_W
CONV_C = 512
CONV_K = 31
CONV_HALO = 16
N_EXPERTS = 16
N_GROUPS = 4
EXPERTS_PER_GROUP = 4
PAIRS_PER_GROUP = 6
N_CLASSES = N_GROUPS * PAIRS_PER_GROUP
CLASS_ROWS = 32
D_EXPERT = 512
N_MOD = 6
GRID_W = 64
ROPE_AXIS_DIM = HEAD_DIM // 2
ROPE_BASE = 10000.0
EPS = 1e-5
LANES = 128
SUBLANES = 8
META_W = LANES
NEG_BIG = -1e30

VMEM_LIMIT = 56 * 1024 * 1024


def _dot(a, b):
    return jnp.dot(a, b, preferred_element_type=F32)


def _dot_nt(a, b):
    return lax.dot_general(a, b, (((1,), (1,)), ((), ())), preferred_element_type=F32)


def _split_bf16(a):
    hi = a.astype(BF16)
    lo = (a - hi.astype(F32)).astype(BF16)
    return hi, lo


def _sigmoid(x):
    return 1.0 / (1.0 + jnp.exp(-x))


def _layer_norm(x, g, b):
    mu = jnp.mean(x, axis=-1, keepdims=True)
    xc = x - mu
    var = jnp.mean(xc * xc, axis=-1, keepdims=True)
    return xc * lax.rsqrt(var + EPS) * g + b


def _params(sem):
    return pltpu.CompilerParams(dimension_semantics=sem, vmem_limit_bytes=VMEM_LIMIT)


def _ada_kernel(c_ref, w_ref, b_ref, o_ref):
    c = c_ref[...]
    a = c * _sigmoid(c)
    a_hi, a_lo = _split_bf16(a)
    w_hi, w_lo = _split_bf16(w_ref[...])
    o_ref[...] = _dot(a_hi, w_hi) + _dot(a_lo, w_hi) + _dot(a_hi, w_lo) + b_ref[...]


def _ada(cs, w, b):
    n = w.shape[1]
    nb = 1536
    return pl.pallas_call(
        _ada_kernel,
        grid=(n // nb,),
        in_specs=[pl.BlockSpec((8, D_MODEL), lambda j: (0, 0)),
                  pl.BlockSpec((D_MODEL, nb), lambda j: (0, j)),
                  pl.BlockSpec((1, nb), lambda j: (0, j))],
        out_specs=pl.BlockSpec((8, nb), lambda j: (0, j)),
        out_shape=jax.ShapeDtypeStruct((8, n), F32),
        compiler_params=_params(("arbitrary",)),
        name="ada",
    )(cs, w, b)


def _inproj_kernel(x_ref, mod_ref, cos_ref, sin_ref, wqk_ref, wvt_ref, wu_ref, wg_ref,
                   q_ref, k1_ref, k2_ref, vt_ref, y_ref, g_ref):
    m = mod_ref[0]
    hx = (x_ref[...] * (1.0 + m[1:2]) + m[0:1]).astype(BF16)

    qk = _dot(hx, wqk_ref[...])
    w = qk.shape[1]
    lane = lax.broadcasted_iota(jnp.int32, qk.shape, 1)
    first_half = (lane % HEAD_DIM) < (HEAD_DIM // 2)
    partner = jnp.where(first_half,
                        pltpu.roll(qk, w - HEAD_DIM // 2, 1),
                        pltpu.roll(qk, HEAD_DIM // 2, 1))
    reps = w // LANES
    cos = jnp.concatenate([cos_ref[...]] * reps, axis=1)
    sin = jnp.concatenate([sin_ref[...]] * reps, axis=1)
    roped = qk * cos + partner * sin
    q_ref[...] = (roped[:, :QK_W] * (HEAD_DIM ** -0.5 * LOG2E)).astype(BF16)
    k = roped[:, QK_W:]
    map1 = (lax.broadcasted_iota(jnp.int32, k.shape, 1) % HEAD_W) < HEAD_DIM
    k1_ref[...] = jnp.where(map1, k, 0.0).astype(BF16)
    k2_ref[...] = jnp.where(map1, 0.0, k).astype(BF16)

    vt_ref[...] = _dot_nt(wvt_ref[...], hx).astype(BF16)

    u = _dot(hx, wu_ref[...])
    y_ref[...] = u[:, :CONV_C] * _sigmoid(u[:, CONV_C:])
    g_ref[...] = _sigmoid(_dot(hx, wg_ref[...])).astype(BF16)


def _inproj(x_all, mod, cos_t, sin_t, wqk, wvt, wu, wg, *, rb, n_lat, seq, batch):
    t = x_all.shape[0]
    nlat = n_lat // rb
    per_seq = seq // rb

    def mod_idx(i):
        return (jnp.where(i < nlat, (i * rb) // seq, batch), 0, 0)

    def rope_idx(i):
        return (jnp.where(i < nlat, i % per_seq, per_seq), 0)

    const = lambda i: (0, 0)
    row = lambda i: (i, 0)
    return pl.pallas_call(
        _inproj_kernel,
        grid=(t // rb,),
        in_specs=[pl.BlockSpec((rb, D_MODEL), row),
                  pl.BlockSpec((1, N_MOD, D_MODEL), mod_idx),
                  pl.BlockSpec((rb, LANES), rope_idx),
                  pl.BlockSpec((rb, LANES), rope_idx),
                  pl.BlockSpec(wqk.shape, const),
                  pl.BlockSpec(wvt.shape, const),
                  pl.BlockSpec(wu.shape, const),
                  pl.BlockSpec(wg.shape, const)],
        out_specs=[pl.BlockSpec((rb, QK_W), row),
                   pl.BlockSpec((rb, QK_W), row),
                   pl.BlockSpec((rb, QK_W), row),
                   pl.BlockSpec((QK_W, rb), lambda i: (0, i)),
                   pl.BlockSpec((rb, CONV_C), row),
                   pl.BlockSpec((rb, 2 * D_MODEL), row)],
        out_shape=[jax.ShapeDtypeStruct((t, QK_W), BF16),
                   jax.ShapeDtypeStruct((t, QK_W), BF16),
                   jax.ShapeDtypeStruct((t, QK_W), BF16),
                   jax.ShapeDtypeStruct((QK_W, t), BF16),
                   jax.ShapeDtypeStruct((t, CONV_C), F32),
                   jax.ShapeDtypeStruct((t, 2 * D_MODEL), BF16)],
        compiler_params=_params(("arbitrary",)),
        name="inproj",
    )(x_all, mod, cos_t, sin_t, wqk, wvt, wu, wg)


ONES_ROWS = 16
ACC_ROWS = HEAD_W + ONES_ROWS
LOG2E = math.log2(math.e)


def _attn_kernel(*refs, lam_init, n_chunks, kc):
    if n_chunks:
        (q_ref, kc1_ref, kc2_ref, vtc_ref, k1_ref, k2_ref, vt_ref, lam_ref, g_ref,
         o_ref, acc1_ref, acc2_ref, sa1_ref, sa2_ref, sb1_ref, sb2_ref) = refs
        key_refs = (k1_ref, k2_ref)
    else:
        (q_ref, kc1_ref, kc2_ref, vtc_ref, lam_ref, g_ref, o_ref, acc1_ref, acc2_ref) = refs
    acc_refs = (acc1_ref, acc2_ref)
    q = q_ref[...]
    qb = q.shape[0]

    def with_ones(vt):
        return jnp.concatenate([vt, jnp.ones((ONES_ROWS, vt.shape[1]), BF16)], axis=0)

    def update(i, s, cmax, m_old, vt_ext, first=False):
        m_new = jnp.maximum(m_old, cmax)
        p = jnp.exp2(s - m_new).astype(BF16)
        pv = _dot(vt_ext, p)
        if first:
            acc_refs[i][...] = pv
        else:
            acc_refs[i][...] = jnp.exp2(m_old - m_new) * acc_refs[i][...] + pv
        return m_new

    vtc_ext = with_ones(vtc_ref[...])
    m_init = jnp.full((1, qb), NEG_BIG, F32)
    n_ctx = kc1_ref.shape[0]
    ctx_scores = []
    for i, kref in enumerate((kc1_ref, kc2_ref)):
        s = _dot_nt(kref[...], q)
        cmax = jnp.max(s, axis=0, keepdims=True)
        if n_chunks:
            (sb1_ref, sb2_ref)[i][0:n_ctx, :] = s
        ctx_scores.append((s, cmax))

    if not n_chunks:
        ms = [update(i, s, cmax, m_init, vtc_ext, first=True) for i, (s, cmax) in enumerate(ctx_scores)]

    if n_chunks:
        def scores(c, s_refs):
            off = pl.multiple_of(c * kc, kc)
            cms = []
            for i in range(2):
                s = _dot_nt(key_refs[i][pl.ds(off, kc), :], q)
                s_refs[i][...] = s
                cms.append(jnp.max(s, axis=0, keepdims=True))
            return tuple(cms)

        def consume(c, s_refs, cms, m):
            off = pl.multiple_of(c * kc, kc)
            vt_ext = with_ones(vt_ref[:, pl.ds(off, kc)])
            return tuple(update(i, s_refs[i][...], cms[i], m[i], vt_ext) for i in range(2))

        sa = (sa1_ref, sa2_ref)
        sb = (sb1_ref, sb2_ref)
        cm_a = scores(0, sa)
        ms = [update(i, sb[i][0:n_ctx, :], cmax, m_init, vtc_ext, first=True)
              for i, (_, cmax) in enumerate(ctx_scores)]

        def body(j, carry):
            m = carry[:2]
            cm_a = carry[2:]
            c = 2 * j
            cm_b = scores(c + 1, sb)
            m = consume(c, sa, cm_a, m)
            cm_a = scores(c + 2, sa)
            m = consume(c + 1, sb, cm_b, m)
            return (*m, *cm_a)

        carry = lax.fori_loop(0, n_chunks // 2 - 1, body, (*ms, *cm_a))
        m, cm_a = carry[:2], carry[2:]
        cm_b = scores(n_chunks - 1, sb)
        m = consume(n_chunks - 2, sa, cm_a, m)
        consume(n_chunks - 1, sb, cm_b, m)

    lv = lam_ref[...]
    lam = (jnp.exp(jnp.sum(lv[0:1] * lv[1:2], axis=1, keepdims=True))
           - jnp.exp(jnp.sum(lv[2:3] * lv[3:4], axis=1, keepdims=True)) + lam_init)
    o1 = acc1_ref[0:HEAD_W, :] / acc1_ref[HEAD_W:HEAD_W + 1, :]
    o2 = acc2_ref[0:HEAD_W, :] / acc2_ref[HEAD_W:HEAD_W + 1, :]
    o_t = o1 - lam * o2
    mean_sq = jnp.mean(o_t * o_t, axis=0, keepdims=True)
    o_t = o_t * lax.rsqrt(mean_sq + EPS) * (g_ref[...] * (1.0 - lam_init))
    o_ref[...] = o_t.T.astype(BF16)


def _attn_latent(q, k1, k2, vt, lamv, g_col, *, lam_init, batch, seq, ctx, qb, kc):
    n_lat = batch * seq
    nq = seq // qb
    ctx0 = n_lat // ctx
    n_chunks = seq // kc
    assert n_chunks >= 2 and n_chunks % 2 == 0 and ctx <= kc
    kern = functools.partial(_attn_kernel, lam_init=lam_init, n_chunks=n_chunks, kc=kc)
    return pl.pallas_call(
        kern,
        grid=(batch, HEADS, nq),
        in_specs=[pl.BlockSpec((qb, HEAD_W), lambda b, h, i: (b * nq + i, h)),
                  pl.BlockSpec((ctx, HEAD_W), lambda b, h, i: (ctx0 + b, h)),
                  pl.BlockSpec((ctx, HEAD_W), lambda b, h, i: (ctx0 + b, h)),
                  pl.BlockSpec((HEAD_W, ctx), lambda b, h, i: (h, ctx0 + b)),
                  pl.BlockSpec((seq, HEAD_W), lambda b, h, i: (b, h)),
                  pl.BlockSpec((seq, HEAD_W), lambda b, h, i: (b, h)),
                  pl.BlockSpec((HEAD_W, seq), lambda b, h, i: (h, b)),
                  pl.BlockSpec((4, HEAD_DIM), lambda b, h, i: (0, 0)),
                  pl.BlockSpec((HEAD_W, 1), lambda b, h, i: (0, 0))],
        out_specs=pl.BlockSpec((qb, HEAD_W), lambda b, h, i: (b * nq + i, h)),
        out_shape=jax.ShapeDtypeStruct((n_lat, QK_W), BF16),
        scratch_shapes=[pltpu.VMEM((ACC_ROWS, qb), F32)] * 2 + [pltpu.VMEM((kc, qb), F32)] * 4,
        compiler_params=_params(("arbitrary", "arbitrary", "arbitrary")),
        name="attn_latent",
    )(q, k1, k2, vt, k1, k2, vt, lamv, g_col)


def _attn_ctx(q, k1, k2, vt, lamv, g_col, *, lam_init, batch, seq, ctx):
    ctx0 = batch * seq // ctx
    kern = functools.partial(_attn_kernel, lam_init=lam_init, n_chunks=0, kc=0)
    return pl.pallas_call(
        kern,
        grid=(batch, HEADS),
        in_specs=[pl.BlockSpec((ctx, HEAD_W), lambda b, h: (ctx0 + b, h)),
                  pl.BlockSpec((ctx, HEAD_W), lambda b, h: (ctx0 + b, h)),
                  pl.BlockSpec((ctx, HEAD_W), lambda b, h: (ctx0 + b, h)),
                  pl.BlockSpec((HEAD_W, ctx), lambda b, h: (h, ctx0 + b)),
                  pl.BlockSpec((4, HEAD_DIM), lambda b, h: (0, 0)),
                  pl.BlockSpec((HEAD_W, 1), lambda b, h: (0, 0))],
        out_specs=pl.BlockSpec((ctx, HEAD_W), lambda b, h: (b, h)),
        out_shape=jax.ShapeDtypeStruct((batch * ctx, QK_W), BF16),
        scratch_shapes=[pltpu.VMEM((ACC_ROWS, ctx), F32)] * 2,
        compiler_params=_params(("arbitrary", "arbitrary")),
        name="attn_ctx",
    )(q, k1, k2, vt, lamv, g_col)


CONV_ROW_TILE = 32


def _conv_kernel(y_ref, prev_ref, next_ref, w_ref, b_ref, g_ref, beta_ref, z_ref, buf_ref, shifted_ref, *,
                 rb, nlat, per_seq):
    i = pl.program_id(0)
    pos = i % per_seq
    first = jnp.logical_or(i >= nlat, pos == 0)
    last = jnp.logical_or(i >= nlat, pos == per_seq - 1)

    buf_ref[CONV_HALO:CONV_HALO + rb, :] = y_ref[...]

    @pl.when(first)
    def _():
        buf_ref[0:CONV_HALO, :] = jnp.zeros((CONV_HALO, CONV_C), F32)

    @pl.when(jnp.logical_not(first))
    def _():
        buf_ref[0:CONV_HALO, :] = prev_ref[...]

    @pl.when(last)
    def _():
        buf_ref[CONV_HALO + rb:, :] = jnp.zeros((CONV_HALO, CONV_C), F32)

    @pl.when(jnp.logical_not(last))
    def _():
        buf_ref[CONV_HALO + rb:, :] = next_ref[...]

    n_sh = shifted_ref.shape[1]
    for r in range(1, SUBLANES):
        shifted_ref[r] = buf_ref[r:r + n_sh, :]

    base = CONV_HALO - CONV_K // 2
    for r0 in range(0, rb, CONV_ROW_TILE):
        acc = jnp.zeros((CONV_ROW_TILE, CONV_C), F32) + b_ref[...]
        for k in range(CONV_K):
            a, r = divmod(base + k, SUBLANES)
            lo = r0 + a * SUBLANES
            rows = buf_ref[lo:lo + CONV_ROW_TILE, :] if r == 0 else shifted_ref[r, lo:lo + CONV_ROW_TILE, :]
            acc = acc + w_ref[k:k + 1, :] * rows
        v = _layer_norm(acc, g_ref[...], beta_ref[...])
        z_ref[r0:r0 + CONV_ROW_TILE, :] = (v * _sigmoid(v)).astype(BF16)


def _conv(y, w, b, g, beta, *, n_rows, n_lat, seq, ctx):
    rb = ctx
    hb = rb // CONV_HALO
    t = y.shape[0]
    n_halo = t // CONV_HALO
    kern = functools.partial(_conv_kernel, rb=rb, nlat=n_lat // rb, per_seq=seq // rb)
    const = lambda i: (0, 0)
    return pl.pallas_call(
        kern,
        grid=(n_rows // rb,),
        in_specs=[pl.BlockSpec((rb, CONV_C), lambda i: (i, 0)),
                  pl.BlockSpec((CONV_HALO, CONV_C), lambda i: (jnp.maximum(i * hb - 1, 0), 0)),
                  pl.BlockSpec((CONV_HALO, CONV_C), lambda i: (jnp.minimum((i + 1) * hb, n_halo - 1), 0)),
                  pl.BlockSpec(w.shape, const),
                  pl.BlockSpec((1, CONV_C), const),
                  pl.BlockSpec((1, CONV_C), const),
                  pl.BlockSpec((1, CONV_C), const)],
        out_specs=pl.BlockSpec((rb, CONV_C), lambda i: (i, 0)),
        out_shape=jax.ShapeDtypeStruct((n_rows, CONV_C), BF16),
        scratch_shapes=[pltpu.VMEM((rb + 2 * CONV_HALO, CONV_C), F32),
                        pltpu.VMEM((SUBLANES, rb + 2 * CONV_HALO - SUBLANES, CONV_C), F32)],
        compiler_params=_params(("arbitrary",)),
        name="conv",
    )(y, y, y, w, b, g, beta)


def _first_index(vals, target):
    idx = jnp.full(target.shape, len(vals) - 1, jnp.int32)
    for j in range(len(vals) - 2, -1, -1):
        idx = jnp.where(vals[j] == target, j, idx)
    return idx


def _route(logits_t):
    mx = jnp.max(logits_t, axis=0, keepdims=True)
    e = jnp.exp(logits_t - mx)
    m1s, m2s, i1s, i2s, scores = [], [], [], [], []
    for g in range(N_GROUPS):
        a = [e[g * EXPERTS_PER_GROUP + j:g * EXPERTS_PER_GROUP + j + 1, :] for j in range(EXPERTS_PER_GROUP)]
        m1 = functools.reduce(jnp.maximum, a)
        i1 = _first_index(a, m1)
        rest = [jnp.where(i1 == j, -1.0, a[j]) for j in range(EXPERTS_PER_GROUP)]
        m2 = functools.reduce(jnp.maximum, rest)
        i2 = _first_index(rest, m2)
        m1s.append(m1); m2s.append(m2); i1s.append(i1); i2s.append(i2); scores.append(m1 + m2)
    gstar = _first_index(scores, functools.reduce(jnp.maximum, scores))

    def pick(xs):
        out = xs[N_GROUPS - 1]
        for g in range(N_GROUPS - 2, -1, -1):
            out = jnp.where(gstar == g, xs[g], out)
        return out

    m1, m2, i1, i2 = pick(m1s), pick(m2s), pick(i1s), pick(i2s)
    tot = m1 + m2
    w1, w2 = m1 / tot, m2 / tot
    lo = jnp.minimum(i1, i2)
    hi = jnp.maximum(i1, i2)
    w_lo = jnp.where(i1 < i2, w1, w2)
    w_hi = jnp.where(i1 < i2, w2, w1)
    pair = jnp.where(lo == 0, 0, jnp.where(lo == 1, 3, 5)) + (hi - lo - 1)
    return gstar * PAIRS_PER_GROUP + pair, w_lo, w_hi


def _merge_kernel(o_ref, octx_ref, z_ref, g_ref, x_ref, mod_ref, wao_ref, wco_ref, wout_ref, lng_ref, lnb_ref,
                  wr_hi_ref, wr_lo_ref, br_ref,
                  x1_ref, hext_ref, meta_ref, cnt_ref, carry_ref, *, alpha, nlat):
    i = pl.program_id(0)

    @pl.when(i == 0)
    def _():
        carry_ref[...] = jnp.zeros(carry_ref.shape, F32)

    m = mod_ref[0]
    att = _dot(jnp.where(i < nlat, o_ref[...], octx_ref[...]), wao_ref[...])
    cnv = _dot(z_ref[...], wco_ref[...])
    gates = g_ref[...].astype(F32)
    mixin = gates[:, :D_MODEL] * att + gates[:, D_MODEL:] * cnv
    mix = _dot(mixin.astype(BF16), wout_ref[...])
    x1 = _layer_norm(alpha * x_ref[...] + m[2:3] * mix, lng_ref[...], lnb_ref[...])
    x1_ref[...] = x1
    h = x1 * (1.0 + m[4:5]) + m[3:4]
    hext_ref[:, :D_MODEL] = h

    h_hi, h_lo = _split_bf16(h)
    logits_t = (_dot_nt(wr_hi_ref[...], h_hi) + _dot_nt(wr_hi_ref[...], h_lo)
                + _dot_nt(wr_lo_ref[...], h_hi) + br_ref[...])
    cls, w_lo, w_hi = _route(logits_t)
    rb = h.shape[0]

    crow = lax.broadcasted_iota(jnp.int32, (CLASS_ROWS, rb), 0)
    onehot = (crow == cls).astype(F32)
    tri = (lax.broadcasted_iota(jnp.int32, (rb, rb), 0)
           < lax.broadcasted_iota(jnp.int32, (rb, rb), 1)).astype(BF16)
    before = _dot(onehot.astype(BF16), tri) + carry_ref[:, 0:1]
    rank = jnp.sum(onehot * before, axis=0, keepdims=True)
    carry_ref[...] = carry_ref[...] + jnp.sum(onehot, axis=1, keepdims=True)
    cnt_ref[...] = carry_ref[...]

    mrow = lax.broadcasted_iota(jnp.int32, (8, rb), 0)
    meta_ref[...] = jnp.where(mrow == 0, cls.astype(F32), jnp.where(mrow == 1, rank, 0.0))

    wrow = lax.broadcasted_iota(jnp.int32, (META_W, rb), 0)
    wts_t = jnp.where(wrow == 0, w_lo, jnp.where(wrow == 1, w_hi, 0.0))
    hext_ref[:, D_MODEL:] = wts_t.T


def _merge(o, o_ctx, z, gates, x_all, mod, wao, wco, wout, lng, lnb, wr_hi, wr_lo, br, *,
           rb, n_rows, n_lat, seq, batch, alpha):
    t = n_rows
    nlat = n_lat // rb

    def mod_idx(i):
        return (jnp.where(i < nlat, (i * rb) // seq, batch), 0, 0)

    const = lambda i: (0, 0)
    row = lambda i: (i, 0)
    kern = functools.partial(_merge_kernel, alpha=alpha, nlat=nlat)
    return pl.pallas_call(
        kern,
        grid=(n_rows // rb,),
        in_specs=[pl.BlockSpec((rb, QK_W), lambda i: (jnp.minimum(i, nlat - 1), 0)),
                  pl.BlockSpec((rb, QK_W), lambda i: (jnp.maximum(i - nlat, 0), 0)),
                  pl.BlockSpec((rb, CONV_C), row),
                  pl.BlockSpec((rb, 2 * D_MODEL), row),
                  pl.BlockSpec((rb, D_MODEL), row),
                  pl.BlockSpec((1, N_MOD, D_MODEL), mod_idx),
                  pl.BlockSpec(wao.shape, const),
                  pl.BlockSpec(wco.shape, const),
                  pl.BlockSpec(wout.shape, const),
                  pl.BlockSpec((1, D_MODEL), const),
                  pl.BlockSpec((1, D_MODEL), const),
                  pl.BlockSpec(wr_hi.shape, const),
                  pl.BlockSpec(wr_lo.shape, const),
                  pl.BlockSpec((N_EXPERTS, 1), const)],
        out_specs=[pl.BlockSpec((rb, D_MODEL), row),
                   pl.BlockSpec((rb, D_MODEL + META_W), row),
                   pl.BlockSpec((8, rb), lambda i: (0, i)),
                   pl.BlockSpec((CLASS_ROWS, LANES), const)],
        out_shape=[jax.ShapeDtypeStruct((t, D_MODEL), F32),
                   jax.ShapeDtypeStruct((t, D_MODEL + META_W), F32),
                   jax.ShapeDtypeStruct((8, t), F32),
                   jax.ShapeDtypeStruct((CLASS_ROWS, LANES), F32)],
        scratch_shapes=[pltpu.VMEM((CLASS_ROWS, LANES), F32)],
        compiler_params=_params(("arbitrary",)),
        name="merge",
    )(o, o_ctx, z, gates, x_all, mod, wao, wco, wout, lng, lnb, wr_hi, wr_lo, br)


def _dispatch_kernel(pos_ref, h_ref, init_ref, out_ref, buf_ref, load_sem, scat_sem, *, rb, n_steps):
    del init_ref
    i = pl.program_id(0)
    slot = i % 2

    def load(step, s):
        return pltpu.make_async_copy(h_ref.at[pl.ds(step * rb, rb)], buf_ref.at[s], load_sem.at[s])

    def wait_scatter(s):
        pltpu.make_async_copy(buf_ref.at[s], out_ref.at[pl.ds(0, rb)], scat_sem.at[s]).wait()

    @pl.when(i == 0)
    def _():
        load(0, 0).start()

    @pl.when(i > 0)
    def _():
        wait_scatter(1 - slot)

    @pl.when(i + 1 < n_steps)
    def _():
        load(i + 1, 1 - slot).start()

    load(i, slot).wait()

    def issue(r, carry):
        pltpu.make_async_copy(buf_ref.at[slot, pl.ds(r, 1)], out_ref.at[pl.ds(pos_ref[0, 0, r], 1)],
                              scat_sem.at[slot]).start()
        return carry

    lax.fori_loop(0, rb, issue, 0, unroll=8)

    @pl.when(i == n_steps - 1)
    def _():
        wait_scatter(slot)


def _dispatch(pos3, hext, init, *, rb, n_rows):
    n_steps = n_rows // rb
    kern = functools.partial(_dispatch_kernel, rb=rb, n_steps=n_steps)
    return pl.pallas_call(
        kern,
        grid=(n_steps,),
        in_specs=[pl.BlockSpec((1, 1, rb), lambda i: (i, 0, 0), memory_space=pltpu.SMEM),
                  pl.BlockSpec(memory_space=pl.ANY),
                  pl.BlockSpec(memory_space=pl.ANY)],
        out_specs=pl.BlockSpec(memory_space=pl.ANY),
        out_shape=jax.ShapeDtypeStruct(init.shape, F32),
        scratch_shapes=[pltpu.VMEM((2, rb, hext.shape[1]), F32), pltpu.SemaphoreType.DMA((2,)),
                        pltpu.SemaphoreType.DMA((2,))],
        input_output_aliases={2: 0},
        compiler_params=_params(("arbitrary",)),
        name="dispatch",
    )(pos3, hext, init)


def _experts_kernel(elo_ref, ehi_ref, valid_ref, x_ref, wgu_lo_ref, wgu_hi_ref, wd_lo_ref, wd_hi_ref, y_ref):
    j = pl.program_id(0)

    @pl.when(valid_ref[j] > 0)
    def _():
        xb = x_ref[:, :D_MODEL].astype(BF16)
        wts = x_ref[:, D_MODEL:]

        def expert(wgu_ref, wd_ref):
            gu = _dot(xb, wgu_ref[0])
            gate = gu[:, :D_EXPERT]
            act = gate * _sigmoid(gate) * gu[:, D_EXPERT:]
            return _dot(act.astype(BF16), wd_ref[0])

        y_ref[...] = (wts[:, 0:1] * expert(wgu_lo_ref, wd_lo_ref)
                      + wts[:, 1:2] * expert(wgu_hi_ref, wd_hi_ref))

    @pl.when(valid_ref[j] == 0)
    def _():
        y_ref[...] = jnp.zeros(y_ref.shape, F32)


def _experts(tile_lo, tile_hi, tile_valid, hs, wgu, wd, *, tm):
    p, width = hs.shape
    grid_spec = pltpu.PrefetchScalarGridSpec(
        num_scalar_prefetch=3,
        grid=(p // tm,),
        in_specs=[pl.BlockSpec((tm, width), lambda j, lo, hi, v: (j, 0)),
                  pl.BlockSpec((1, D_MODEL, 2 * D_EXPERT), lambda j, lo, hi, v: (lo[j], 0, 0)),
                  pl.BlockSpec((1, D_MODEL, 2 * D_EXPERT), lambda j, lo, hi, v: (hi[j], 0, 0)),
                  pl.BlockSpec((1, D_EXPERT, D_MODEL), lambda j, lo, hi, v: (lo[j], 0, 0)),
                  pl.BlockSpec((1, D_EXPERT, D_MODEL), lambda j, lo, hi, v: (hi[j], 0, 0))],
        out_specs=pl.BlockSpec((tm, D_MODEL), lambda j, lo, hi, v: (j, 0)),
    )
    return pl.pallas_call(
        _experts_kernel,
        grid_spec=grid_spec,
        out_shape=jax.ShapeDtypeStruct((p, D_MODEL), F32),
        compiler_params=_params(("arbitrary",)),
        name="experts",
    )(tile_lo, tile_hi, tile_valid, hs, wgu, wgu, wd, wd)


def _combine_kernel(pos_ref, pos_next_ref, ys_ref, x1_ref, mod_ref, lng_ref, lnb_ref, out_ref, ybuf_ref, sem, *,
                    rb, alpha, n_steps):
    i = pl.program_id(0)
    slot = i % 2

    def gather(p_ref, dst_slot):
        def issue(r, carry):
            pltpu.make_async_copy(ys_ref.at[pl.ds(p_ref[0, 0, r], 1)], ybuf_ref.at[dst_slot, pl.ds(r, 1)],
                                  sem.at[dst_slot]).start()
            return carry
        lax.fori_loop(0, rb, issue, 0, unroll=8)

    @pl.when(i == 0)
    def _():
        gather(pos_ref, 0)

    @pl.when(i + 1 < n_steps)
    def _():
        gather(pos_next_ref, 1 - slot)

    pltpu.make_async_copy(ys_ref.at[pl.ds(0, rb)], ybuf_ref.at[slot], sem.at[slot]).wait()

    m = mod_ref[0]
    out_ref[...] = _layer_norm(alpha * x1_ref[...] + m[5:6] * ybuf_ref[slot], lng_ref[...], lnb_ref[...])


def _combine(pos3, ys, x1, mod, lng, lnb, *, rb, n_rows, n_lat, seq, batch, alpha):
    nlat = n_lat // rb

    def mod_idx(i):
        return (jnp.where(i < nlat, (i * rb) // seq, batch), 0, 0)

    const = lambda i: (0, 0)
    n_steps = n_rows // rb
    kern = functools.partial(_combine_kernel, rb=rb, alpha=alpha, n_steps=n_steps)
    return pl.pallas_call(
        kern,
        grid=(n_steps,),
        in_specs=[pl.BlockSpec((1, 1, rb), lambda i: (i, 0, 0), memory_space=pltpu.SMEM),
                  pl.BlockSpec((1, 1, rb), lambda i: (jnp.minimum(i + 1, n_steps - 1), 0, 0),
                               memory_space=pltpu.SMEM),
                  pl.BlockSpec(memory_space=pl.ANY),
                  pl.BlockSpec((rb, D_MODEL), lambda i: (i, 0)),
                  pl.BlockSpec((1, N_MOD, D_MODEL), mod_idx),
                  pl.BlockSpec((1, D_MODEL), const),
                  pl.BlockSpec((1, D_MODEL), const)],
        out_specs=pl.BlockSpec((rb, D_MODEL), lambda i: (i, 0)),
        out_shape=jax.ShapeDtypeStruct((n_rows, D_MODEL), F32),
        scratch_shapes=[pltpu.VMEM((2, rb, D_MODEL), F32), pltpu.SemaphoreType.DMA((2,))],
        compiler_params=_params(("arbitrary",)),
        name="combine",
    )(pos3, pos3, ys, x1, mod, lng, lnb)


def _rope_tables(seq, pad_rows):
    rows = seq // GRID_W
    row = jnp.repeat(jnp.arange(rows, dtype=F32), GRID_W)
    col = jnp.tile(jnp.arange(GRID_W, dtype=F32), rows)
    inv_freq = ROPE_BASE ** (-jnp.arange(0, ROPE_AXIS_DIM, 2, dtype=F32) / ROPE_AXIS_DIM)
    ang = jnp.concatenate([row[:, None] * inv_freq, col[:, None] * inv_freq], -1)
    cos, sin = jnp.cos(ang), jnp.sin(ang)
    cos64 = jnp.concatenate([cos, cos], -1)
    sin64 = jnp.concatenate([-sin, sin], -1)
    cos_t = jnp.concatenate([cos64, cos64], -1)
    sin_t = jnp.concatenate([sin64, sin64], -1)
    cos_t = jnp.concatenate([cos_t, jnp.ones((pad_rows, LANES), F32)], 0)
    sin_t = jnp.concatenate([sin_t, jnp.zeros((pad_rows, LANES), F32)], 0)
    return cos_t, sin_t


def _class_tables():
    lo, hi = [], []
    for g in range(N_GROUPS):
        for a in range(EXPERTS_PER_GROUP):
            for b in range(a + 1, EXPERTS_PER_GROUP):
                lo.append(g * EXPERTS_PER_GROUP + a)
                hi.append(g * EXPERTS_PER_GROUP + b)
    return jnp.array(lo, jnp.int32), jnp.array(hi, jnp.int32)


def kernel(x, c, ctx, c_ctx, w_ada, b_ada, w_in, lam_q1, lam_k1, lam_q2, lam_k2, subln_g, w_attn_o, conv_w, conv_b, conv_ln_g, conv_ln_b, w_conv_o, w_out, ln1_g, ln1_b, w_router, b_router, w_e_gate, w_e_up, w_e_down, ln2_g, ln2_b):
    batch, seq, d = x.shape
    n_ctx = ctx.shape[1]
    depth = w_ada.shape[0]
    n_lat = batch * seq
    t = n_lat + batch * n_ctx
    alpha = (2 * depth) ** 0.25

    rb = 512 if (seq % 512 == 0 and (batch * n_ctx) % 512 == 0) else 256
    rbd = 256
    tm = 256
    qb = 512 if seq % 512 == 0 else 256
    kc = 512 if seq % 512 == 0 else 256
    assert batch + 1 <= 8 and seq % rb == 0 and seq % n_ctx == 0 and n_ctx % CONV_HALO == 0

    q_end, k_end, v_end = QK_W, 2 * QK_W, 3 * QK_W
    u_end = v_end + 2 * CONV_C
    perm64 = jnp.concatenate([jnp.arange(0, HEAD_DIM, 2), jnp.arange(1, HEAD_DIM, 2)])
    perm_qk = (jnp.arange(2 * QK_W // HEAD_DIM)[:, None] * HEAD_DIM + perm64[None, :]).reshape(-1)

    cos_t, sin_t = _rope_tables(seq, rb)
    cls_lo, cls_hi = _class_tables()
    p_rows = ((t + tm - 1) // tm + N_CLASSES) * tm

    cs = jnp.zeros((8, d), F32).at[:batch].set(c).at[batch].set(c_ctx)
    x_all = jnp.concatenate([x.reshape(n_lat, d), ctx.reshape(batch * n_ctx, d)], 0)
    wr_hi, wr_lo = _split_bf16(w_router.T)
    br = b_router.reshape(N_EXPERTS, 1)

    for l in range(depth):
        last = l == depth - 1
        lam_init = 0.8 - 0.6 * math.exp(-0.3 * l)
        n_rows = n_lat if last else t

        mod = _ada(cs, w_ada[l], b_ada[l].reshape(1, -1)).reshape(8, N_MOD, d)

        w_l = w_in[l]
        wqk = w_l[:, :k_end][:, perm_qk].astype(BF16)
        wvt = w_l[:, k_end:v_end].T.astype(BF16)
        wu = w_l[:, v_end:u_end].astype(BF16)
        wg = w_l[:, u_end:].astype(BF16)
        q, k1, k2, vt, y, gates = _inproj(x_all, mod, cos_t, sin_t, wqk, wvt, wu, wg,
                                          rb=rb, n_lat=n_lat, seq=seq, batch=batch)

        lamv = jnp.stack([lam_q1[l], lam_k1[l], lam_q2[l], lam_k2[l]])
        g_col = subln_g[l].reshape(HEAD_W, 1)
        o = _attn_latent(q, k1, k2, vt, lamv, g_col, lam_init=lam_init, batch=batch, seq=seq,
                         ctx=n_ctx, qb=qb, kc=kc)
        o_ctx = o if last else _attn_ctx(q, k1, k2, vt, lamv, g_col, lam_init=lam_init, batch=batch, seq=seq,
                                         ctx=n_ctx)

        cw = jnp.concatenate([conv_w[l], jnp.zeros((1, CONV_C), F32)], 0)
        z = _conv(y, cw, conv_b[l].reshape(1, -1), conv_ln_g[l].reshape(1, -1), conv_ln_b[l].reshape(1, -1),
                  n_rows=n_rows, n_lat=n_lat, seq=seq, ctx=n_ctx)

        x1, hext, meta, counts = _merge(
            o, o_ctx, z, gates, x_all, mod, w_attn_o[l].astype(BF16), w_conv_o[l].astype(BF16), w_out[l].astype(BF16),
            ln1_g[l].reshape(1, -1), ln1_b[l].reshape(1, -1), wr_hi, wr_lo, br,
            rb=rb, n_rows=n_rows, n_lat=n_lat, seq=seq, batch=batch, alpha=alpha)

        cnt = counts[:N_CLASSES, 0].astype(jnp.int32)
        padded = ((cnt + tm - 1) // tm) * tm
        ends = jnp.cumsum(padded)
        starts = ends - padded
        cls = meta[0, :n_rows].astype(jnp.int32)
        pos = starts[cls] + meta[1, :n_rows].astype(jnp.int32)
        pos3 = pos.reshape(n_rows // rbd, 1, rbd)
        tile_row0 = jnp.arange(p_rows // tm, dtype=jnp.int32) * tm
        tile_cls = jnp.minimum(jnp.sum(tile_row0[:, None] >= ends[None, :], axis=1), N_CLASSES - 1)
        tile_valid = (tile_row0 < ends[-1]).astype(jnp.int32)
        tile_lo, tile_hi = cls_lo[tile_cls], cls_hi[tile_cls]

        hs = _dispatch(pos3, hext, jnp.zeros((p_rows, d + META_W), F32), rb=rbd, n_rows=n_rows)
        wgu = jnp.concatenate([w_e_gate[l], w_e_up[l]], -1).astype(BF16)
        ys = _experts(tile_lo, tile_hi, tile_valid, hs, wgu, w_e_down[l].astype(BF16), tm=tm)
        x_all = _combine(pos3, ys, x1, mod, ln2_g[l].reshape(1, -1), ln2_b[l].reshape(1, -1),
                         rb=rbd, n_rows=n_rows, n_lat=n_lat, seq=seq, batch=batch, alpha=alpha)

    return x_all.reshape(batch, seq, d)
```

```python
import functools
import math

import jax
import jax.numpy as jnp
from jax import lax
from jax.experimental import pallas as pl
from jax.experimental.pallas import tpu as pltpu

F32 = jnp.float32
BF16 = jnp.bfloat16

D_MODEL = 1024
HEADS = 4
HEAD_DIM = 64
HEAD_W = 2 * HEAD_DIM
QK_W = HEADS * HEAD_W
CONV_C = 512
CONV_K = 31
CONV_HALO = 16
N_EXPERTS = 16
N_GROUPS = 4
EXPERTS_PER_GROUP = 4
PAIRS_PER_GROUP = 6
N_CLASSES = N_GROUPS * PAIRS_PER_GROUP
CLASS_ROWS = 32
D_EXPERT = 512
N_MOD = 6
GRID_W = 64
ROPE_AXIS_DIM = HEAD_DIM // 2
ROPE_BASE = 10000.0
EPS = 1e-5
LANES = 128
SUBLANES = 8
META_W = LANES
NEG_BIG = -1e30

VMEM_LIMIT = 56 * 1024 * 1024


def _dot(a, b):
    return jnp.dot(a, b, preferred_element_type=F32)


def _dot_nt(a, b):
    return lax.dot_general(a, b, (((1,), (1,)), ((), ())), preferred_element_type=F32)


def _split_bf16(a):
    hi = a.astype(BF16)
    lo = (a - hi.astype(F32)).astype(BF16)
    return hi, lo


def _sigmoid(x):
    return 1.0 / (1.0 + jnp.exp(-x))


def _layer_norm(x, g, b):
    mu = jnp.mean(x, axis=-1, keepdims=True)
    xc = x - mu
    var = jnp.mean(xc * xc, axis=-1, keepdims=True)
    return xc * lax.rsqrt(var + EPS) * g + b


def _params(sem):
    return pltpu.CompilerParams(dimension_semantics=sem, vmem_limit_bytes=VMEM_LIMIT)


def _ada_kernel(c_ref, w_ref, b_ref, o_ref):
    c = c_ref[...]
    a = c * _sigmoid(c)
    a_hi, a_lo = _split_bf16(a)
    w_hi, w_lo = _split_bf16(w_ref[...])
    o_ref[...] = _dot(a_hi, w_hi) + _dot(a_lo, w_hi) + _dot(a_hi, w_lo) + b_ref[...]


def _ada(cs, w, b):
    n = w.shape[1]
    nb = 1536
    return pl.pallas_call(
        _ada_kernel,
        grid=(n // nb,),
        in_specs=[pl.BlockSpec((8, D_MODEL), lambda j: (0, 0)),
                  pl.BlockSpec((D_MODEL, nb), lambda j: (0, j)),
                  pl.BlockSpec((1, nb), lambda j: (0, j))],
        out_specs=pl.BlockSpec((8, nb), lambda j: (0, j)),
        out_shape=jax.ShapeDtypeStruct((8, n), F32),
        compiler_params=_params(("arbitrary",)),
        name="ada",
    )(cs, w, b)


def _inproj_kernel(x_ref, xc_ref, mod_ref, cos_ref, sin_ref, wqk_ref, wvt_ref, wu_ref, wg_ref,
                   q_ref, k1_ref, k2_ref, vt_ref, y_ref, g_ref, *, nlat):
    m = mod_ref[0]
    x = jnp.where(pl.program_id(0) < nlat, x_ref[...], xc_ref[...])
    hx = (x * (1.0 + m[1:2]) + m[0:1]).astype(BF16)

    qk = _dot(hx, wqk_ref[...])
    w = qk.shape[1]
    lane = lax.broadcasted_iota(jnp.int32, qk.shape, 1)
    first_half = (lane % HEAD_DIM) < (HEAD_DIM // 2)
    partner = jnp.where(first_half,
                        pltpu.roll(qk, w - HEAD_DIM // 2, 1),
                        pltpu.roll(qk, HEAD_DIM // 2, 1))
    reps = w // LANES
    cos = jnp.concatenate([cos_ref[...]] * reps, axis=1)
    sin = jnp.concatenate([sin_ref[...]] * reps, axis=1)
    roped = qk * cos + partner * sin
    q_ref[...] = (roped[:, :QK_W] * (HEAD_DIM ** -0.5 * LOG2E)).astype(BF16)
    k = roped[:, QK_W:]
    map1 = (lax.broadcasted_iota(jnp.int32, k.shape, 1) % HEAD_W) < HEAD_DIM
    k1_ref[...] = jnp.where(map1, k, 0.0).astype(BF16)
    k2_ref[...] = jnp.where(map1, 0.0, k).astype(BF16)

    vt_ref[...] = _dot_nt(wvt_ref[...], hx).astype(BF16)

    u = _dot(hx, wu_ref[...])
    y_ref[...] = u[:, :CONV_C] * _sigmoid(u[:, CONV_C:])
    g_ref[...] = _sigmoid(_dot(hx, wg_ref[...])).astype(BF16)


def _token_specs(rb, nlat, ctx_off):
    return (pl.BlockSpec((rb, D_MODEL), lambda i: (jnp.minimum(i, nlat - 1), 0)),
            pl.BlockSpec((rb, D_MODEL), lambda i: (ctx_off + jnp.maximum(i - nlat, 0), 0)))


def _inproj(x_lat, x_ctx, ctx_off, mod, cos_t, sin_t, wqk, wvt, wu, wg, *, rb, n_rows, n_lat, seq, batch):
    t = n_rows
    nlat = n_lat // rb
    per_seq = seq // rb

    def mod_idx(i):
        return (jnp.where(i < nlat, (i * rb) // seq, batch), 0, 0)

    def rope_idx(i):
        return (jnp.where(i < nlat, i % per_seq, per_seq), 0)

    const = lambda i: (0, 0)
    row = lambda i: (i, 0)
    return pl.pallas_call(
        functools.partial(_inproj_kernel, nlat=nlat),
        grid=(t // rb,),
        in_specs=[*_token_specs(rb, nlat, ctx_off),
                  pl.BlockSpec((1, N_MOD, D_MODEL), mod_idx),
                  pl.BlockSpec((rb, LANES), rope_idx),
                  pl.BlockSpec((rb, LANES), rope_idx),
                  pl.BlockSpec(wqk.shape, const),
                  pl.BlockSpec(wvt.shape, const),
                  pl.BlockSpec(wu.shape, const),
                  pl.BlockSpec(wg.shape, const)],
        out_specs=[pl.BlockSpec((rb, QK_W), row),
                   pl.BlockSpec((rb, QK_W), row),
                   pl.BlockSpec((rb, QK_W), row),
                   pl.BlockSpec((QK_W, rb), lambda i: (0, i)),
                   pl.BlockSpec((rb, CONV_C), row),
                   pl.BlockSpec((rb, 2 * D_MODEL), row)],
        out_shape=[jax.ShapeDtypeStruct((t, QK_W), BF16),
                   jax.ShapeDtypeStruct((t, QK_W), BF16),
                   jax.ShapeDtypeStruct((t, QK_W), BF16),
                   jax.ShapeDtypeStruct((QK_W, t), BF16),
                   jax.ShapeDtypeStruct((t, CONV_C), F32),
                   jax.ShapeDtypeStruct((t, 2 * D_MODEL), BF16)],
        compiler_params=_params(("arbitrary",)),
        name="inproj",
    )(x_lat, x_ctx, mod, cos_t, sin_t, wqk, wvt, wu, wg)


ONES_ROWS = 16
ACC_ROWS = HEAD_W + ONES_ROWS
LOG2E = math.log2(math.e)


def _attn_kernel(*refs, lam_init, n_chunks, kc):
    if n_chunks:
        (q_ref, kc1_ref, kc2_ref, vtc_ref, k1_ref, k2_ref, vt_ref, lam_ref, g_ref,
         o_ref, acc1_ref, acc2_ref, sa1_ref, sa2_ref, sb1_ref, sb2_ref) = refs
        key_refs = (k1_ref, k2_ref)
    else:
        (q_ref, kc1_ref, kc2_ref, vtc_ref, lam_ref, g_ref, o_ref, acc1_ref, acc2_ref) = refs
    acc_refs = (acc1_ref, acc2_ref)
    q = q_ref[...]
    qb = q.shape[0]

    def with_ones(vt):
        return jnp.concatenate([vt, jnp.ones((ONES_ROWS, vt.shape[1]), BF16)], axis=0)

    def update(i, s, cmax, m_old, vt_ext, first=False):
        m_new = jnp.maximum(m_old, cmax)
        p = jnp.exp2(s - m_new).astype(BF16)
        pv = _dot(vt_ext, p)
        if first:
            acc_refs[i][...] = pv
        else:
            acc_refs[i][...] = jnp.exp2(m_old - m_new) * acc_refs[i][...] + pv
        return m_new

    vtc_ext = with_ones(vtc_ref[...])
    m_init = jnp.full((1, qb), NEG_BIG, F32)
    n_ctx = kc1_ref.shape[0]
    ctx_scores = []
    for i, kref in enumerate((kc1_ref, kc2_ref)):
        s = _dot_nt(kref[...], q)
        cmax = jnp.max(s, axis=0, keepdims=True)
        if n_chunks:
            (sb1_ref, sb2_ref)[i][0:n_ctx, :] = s
        ctx_scores.append((s, cmax))

    if not n_chunks:
        ms = [update(i, s, cmax, m_init, vtc_ext, first=True) for i, (s, cmax) in enumerate(ctx_scores)]

    if n_chunks:
        def scores(c, s_refs):
            off = pl.multiple_of(c * kc, kc)
            cms = []
            for i in range(2):
                s = _dot_nt(key_refs[i][pl.ds(off, kc), :], q)
                s_refs[i][...] = s
                cms.append(jnp.max(s, axis=0, keepdims=True))
            return tuple(cms)

        def consume(c, s_refs, cms, m):
            off = pl.multiple_of(c * kc, kc)
            vt_ext = with_ones(vt_ref[:, pl.ds(off, kc)])
            return tuple(update(i, s_refs[i][...], cms[i], m[i], vt_ext) for i in range(2))

        sa = (sa1_ref, sa2_ref)
        sb = (sb1_ref, sb2_ref)
        cm_a = scores(0, sa)
        ms = [update(i, sb[i][0:n_ctx, :], cmax, m_init, vtc_ext, first=True)
              for i, (_, cmax) in enumerate(ctx_scores)]

        def body(j, carry):
            m = carry[:2]
            cm_a = carry[2:]
            c = 2 * j
            cm_b = scores(c + 1, sb)
            m = consume(c, sa, cm_a, m)
            cm_a = scores(c + 2, sa)
            m = consume(c + 1, sb, cm_b, m)
            return (*m, *cm_a)

        carry = lax.fori_loop(0, n_chunks // 2 - 1, body, (*ms, *cm_a))
        m, cm_a = carry[:2], carry[2:]
        cm_b = scores(n_chunks - 1, sb)
        m = consume(n_chunks - 2, sa, cm_a, m)
        consume(n_chunks - 1, sb, cm_b, m)

    lv = lam_ref[...]
    lam = (jnp.exp(jnp.sum(lv[0:1] * lv[1:2], axis=1, keepdims=True))
           - jnp.exp(jnp.sum(lv[2:3] * lv[3:4], axis=1, keepdims=True)) + lam_init)
    o1 = acc1_ref[0:HEAD_W, :] / acc1_ref[HEAD_W:HEAD_W + 1, :]
    o2 = acc2_ref[0:HEAD_W, :] / acc2_ref[HEAD_W:HEAD_W + 1, :]
    o_t = o1 - lam * o2
    mean_sq = jnp.mean(o_t * o_t, axis=0, keepdims=True)
    o_t = o_t * lax.rsqrt(mean_sq + EPS) * (g_ref[...] * (1.0 - lam_init))
    o_ref[...] = o_t.T.astype(BF16)


def _attn_latent(q, k1, k2, vt, lamv, g_col, *, lam_init, batch, seq, ctx, qb, kc):
    n_lat = batch * seq
    nq = seq // qb
    ctx0 = n_lat // ctx
    n_chunks = seq // kc
    assert n_chunks >= 2 and n_chunks % 2 == 0 and ctx <= kc
    kern = functools.partial(_attn_kernel, lam_init=lam_init, n_chunks=n_chunks, kc=kc)
    return pl.pallas_call(
        kern,
        grid=(batch, HEADS, nq),
        in_specs=[pl.BlockSpec((qb, HEAD_W), lambda b, h, i: (b * nq + i, h)),
                  pl.BlockSpec((ctx, HEAD_W), lambda b, h, i: (ctx0 + b, h)),
                  pl.BlockSpec((ctx, HEAD_W), lambda b, h, i: (ctx0 + b, h)),
                  pl.BlockSpec((HEAD_W, ctx), lambda b, h, i: (h, ctx0 + b)),
                  pl.BlockSpec((seq, HEAD_W), lambda b, h, i: (b, h)),
                  pl.BlockSpec((seq, HEAD_W), lambda b, h, i: (b, h)),
                  pl.BlockSpec((HEAD_W, seq), lambda b, h, i: (h, b)),
                  pl.BlockSpec((4, HEAD_DIM), lambda b, h, i: (0, 0)),
                  pl.BlockSpec((HEAD_W, 1), lambda b, h, i: (0, 0))],
        out_specs=pl.BlockSpec((qb, HEAD_W), lambda b, h, i: (b * nq + i, h)),
        out_shape=jax.ShapeDtypeStruct((n_lat, QK_W), BF16),
        scratch_shapes=[pltpu.VMEM((ACC_ROWS, qb), F32)] * 2 + [pltpu.VMEM((kc, qb), F32)] * 4,
        compiler_params=_params(("arbitrary", "arbitrary", "arbitrary")),
        name="attn_latent",
    )(q, k1, k2, vt, k1, k2, vt, lamv, g_col)


def _attn_ctx(q, k1, k2, vt, lamv, g_col, *, lam_init, batch, seq, ctx):
    ctx0 = batch * seq // ctx
    kern = functools.partial(_attn_kernel, lam_init=lam_init, n_chunks=0, kc=0)
    return pl.pallas_call(
        kern,
        grid=(batch, HEADS),
        in_specs=[pl.BlockSpec((ctx, HEAD_W), lambda b, h: (ctx0 + b, h)),
                  pl.BlockSpec((ctx, HEAD_W), lambda b, h: (ctx0 + b, h)),
                  pl.BlockSpec((ctx, HEAD_W), lambda b, h: (ctx0 + b, h)),
                  pl.BlockSpec((HEAD_W, ctx), lambda b, h: (h, ctx0 + b)),
                  pl.BlockSpec((4, HEAD_DIM), lambda b, h: (0, 0)),
                  pl.BlockSpec((HEAD_W, 1), lambda b, h: (0, 0))],
        out_specs=pl.BlockSpec((ctx, HEAD_W), lambda b, h: (b, h)),
        out_shape=jax.ShapeDtypeStruct((batch * ctx, QK_W), BF16),
        scratch_shapes=[pltpu.VMEM((ACC_ROWS, ctx), F32)] * 2,
        compiler_params=_params(("arbitrary", "arbitrary")),
        name="attn_ctx",
    )(q, k1, k2, vt, lamv, g_col)


CONV_ROW_TILE = 64


def _conv_kernel(y_ref, prev_ref, next_ref, w_ref, b_ref, g_ref, beta_ref, z_ref, buf_ref, shifted_ref, *,
                 rb, nlat, per_seq):
    i = pl.program_id(0)
    pos = i % per_seq
    first = jnp.logical_or(i >= nlat, pos == 0)
    last = jnp.logical_or(i >= nlat, pos == per_seq - 1)

    buf_ref[CONV_HALO:CONV_HALO + rb, :] = y_ref[...]

    @pl.when(first)
    def _():
        buf_ref[0:CONV_HALO, :] = jnp.zeros((CONV_HALO, CONV_C), F32)

    @pl.when(jnp.logical_not(first))
    def _():
        buf_ref[0:CONV_HALO, :] = prev_ref[...]

    @pl.when(last)
    def _():
        buf_ref[CONV_HALO + rb:, :] = jnp.zeros((CONV_HALO, CONV_C), F32)

    @pl.when(jnp.logical_not(last))
    def _():
        buf_ref[CONV_HALO + rb:, :] = next_ref[...]

    n_sh = shifted_ref.shape[1]
    for r in range(1, SUBLANES):
        shifted_ref[r] = buf_ref[r:r + n_sh, :]

    base = CONV_HALO - CONV_K // 2
    for r0 in range(0, rb, CONV_ROW_TILE):
        acc = jnp.zeros((CONV_ROW_TILE, CONV_C), F32) + b_ref[...]
        for k in range(CONV_K):
            a, r = divmod(base + k, SUBLANES)
            lo = r0 + a * SUBLANES
            rows = buf_ref[lo:lo + CONV_ROW_TILE, :] if r == 0 else shifted_ref[r, lo:lo + CONV_ROW_TILE, :]
            acc = acc + w_ref[k:k + 1, :] * rows
        v = _layer_norm(acc, g_ref[...], beta_ref[...])
        z_ref[r0:r0 + CONV_ROW_TILE, :] = (v * _sigmoid(v)).astype(BF16)


def _conv(y, w, b, g, beta, *, n_rows, n_lat, seq, ctx):
    rb = ctx
    hb = rb // CONV_HALO
    t = y.shape[0]
    n_halo = t // CONV_HALO
    kern = functools.partial(_conv_kernel, rb=rb, nlat=n_lat // rb, per_seq=seq // rb)
    const = lambda i: (0, 0)
    return pl.pallas_call(
        kern,
        grid=(n_rows // rb,),
        in_specs=[pl.BlockSpec((rb, CONV_C), lambda i: (i, 0)),
                  pl.BlockSpec((CONV_HALO, CONV_C), lambda i: (jnp.maximum(i * hb - 1, 0), 0)),
                  pl.BlockSpec((CONV_HALO, CONV_C), lambda i: (jnp.minimum((i + 1) * hb, n_halo - 1), 0)),
                  pl.BlockSpec(w.shape, const),
                  pl.BlockSpec((1, CONV_C), const),
                  pl.BlockSpec((1, CONV_C), const),
                  pl.BlockSpec((1, CONV_C), const)],
        out_specs=pl.BlockSpec((rb, CONV_C), lambda i: (i, 0)),
        out_shape=jax.ShapeDtypeStruct((n_rows, CONV_C), BF16),
        scratch_shapes=[pltpu.VMEM((rb + 2 * CONV_HALO, CONV_C), F32),
                        pltpu.VMEM((SUBLANES, rb + 2 * CONV_HALO - SUBLANES, CONV_C), F32)],
        compiler_params=_params(("arbitrary",)),
        name="conv",
    )(y, y, y, w, b, g, beta)


def _first_index(vals, target):
    idx = jnp.full(target.shape, len(vals) - 1, jnp.int32)
    for j in range(len(vals) - 2, -1, -1):
        idx = jnp.where(vals[j] == target, j, idx)
    return idx


def _route(logits_t):
    mx = jnp.max(logits_t, axis=0, keepdims=True)
    e = jnp.exp(logits_t - mx)
    m1s, m2s, i1s, i2s, scores = [], [], [], [], []
    for g in range(N_GROUPS):
        a = [e[g * EXPERTS_PER_GROUP + j:g * EXPERTS_PER_GROUP + j + 1, :] for j in range(EXPERTS_PER_GROUP)]
        m1 = functools.reduce(jnp.maximum, a)
        i1 = _first_index(a, m1)
        rest = [jnp.where(i1 == j, -1.0, a[j]) for j in range(EXPERTS_PER_GROUP)]
        m2 = functools.reduce(jnp.maximum, rest)
        i2 = _first_index(rest, m2)
        m1s.append(m1); m2s.append(m2); i1s.append(i1); i2s.append(i2); scores.append(m1 + m2)
    gstar = _first_index(scores, functools.reduce(jnp.maximum, scores))

    def pick(xs):
        out = xs[N_GROUPS - 1]
        for g in range(N_GROUPS - 2, -1, -1):
            out = jnp.where(gstar == g, xs[g], out)
        return out

    m1, m2, i1, i2 = pick(m1s), pick(m2s), pick(i1s), pick(i2s)
    tot = m1 + m2
    w1, w2 = m1 / tot, m2 / tot
    lo = jnp.minimum(i1, i2)
    hi = jnp.maximum(i1, i2)
    w_lo = jnp.where(i1 < i2, w1, w2)
    w_hi = jnp.where(i1 < i2, w2, w1)
    pair = jnp.where(lo == 0, 0, jnp.where(lo == 1, 3, 5)) + (hi - lo - 1)
    return gstar * PAIRS_PER_GROUP + pair, w_lo, w_hi


def _merge_kernel(o_ref, octx_ref, z_ref, g_ref, x_ref, xc_ref, mod_ref, wao_ref, wco_ref, wout_ref, lng_ref, lnb_ref,
                  wr_hi_ref, wr_lo_ref, br_ref,
                  x1_ref, hext_ref, meta_ref, cnt_ref, carry_ref, *, alpha, nlat):
    i = pl.program_id(0)

    @pl.when(i == 0)
    def _():
        carry_ref[...] = jnp.zeros(carry_ref.shape, F32)

    m = mod_ref[0]
    att = _dot(jnp.where(i < nlat, o_ref[...], octx_ref[...]), wao_ref[...])
    cnv = _dot(z_ref[...], wco_ref[...])
    gates = g_ref[...].astype(F32)
    mixin = gates[:, :D_MODEL] * att + gates[:, D_MODEL:] * cnv
    mix = _dot(mixin.astype(BF16), wout_ref[...])
    x = jnp.where(i < nlat, x_ref[...], xc_ref[...])
    x1 = _layer_norm(alpha * x + m[2:3] * mix, lng_ref[...], lnb_ref[...])
    x1_ref[...] = x1
    h = x1 * (1.0 + m[4:5]) + m[3:4]
    hext_ref[:, :D_MODEL] = h

    h_hi, h_lo = _split_bf16(h)
    logits_t = (_dot_nt(wr_hi_ref[...], h_hi) + _dot_nt(wr_hi_ref[...], h_lo)
                + _dot_nt(wr_lo_ref[...], h_hi) + br_ref[...])
    cls, w_lo, w_hi = _route(logits_t)
    rb = h.shape[0]

    crow = lax.broadcasted_iota(jnp.int32, (CLASS_ROWS, rb), 0)
    onehot = (crow == cls).astype(F32)
    tri = (lax.broadcasted_iota(jnp.int32, (rb, rb), 0)
           < lax.broadcasted_iota(jnp.int32, (rb, rb), 1)).astype(BF16)
    before = _dot(onehot.astype(BF16), tri) + carry_ref[:, 0:1]
    rank = jnp.sum(onehot * before, axis=0, keepdims=True)
    carry_ref[...] = carry_ref[...] + jnp.sum(onehot, axis=1, keepdims=True)
    cnt_ref[...] = carry_ref[...]

    mrow = lax.broadcasted_iota(jnp.int32, (8, rb), 0)
    meta_ref[...] = jnp.where(mrow == 0, cls.astype(F32), jnp.where(mrow == 1, rank, 0.0))

    wrow = lax.broadcasted_iota(jnp.int32, (META_W, rb), 0)
    wts_t = jnp.where(wrow == 0, w_lo, jnp.where(wrow == 1, w_hi, 0.0))
    hext_ref[:, D_MODEL:] = wts_t.T


def _merge(o, o_ctx, z, gates, x_lat, x_ctx, ctx_off, mod, wao, wco, wout, lng, lnb, wr_hi, wr_lo, br, *,
           rb, n_rows, n_lat, seq, batch, alpha):
    t = n_rows
    nlat = n_lat // rb

    def mod_idx(i):
        return (jnp.where(i < nlat, (i * rb) // seq, batch), 0, 0)

    const = lambda i: (0, 0)
    row = lambda i: (i, 0)
    kern = functools.partial(_merge_kernel, alpha=alpha, nlat=nlat)
    return pl.pallas_call(
        kern,
        grid=(n_rows // rb,),
        in_specs=[pl.BlockSpec((rb, QK_W), lambda i: (jnp.minimum(i, nlat - 1), 0)),
                  pl.BlockSpec((rb, QK_W), lambda i: (jnp.maximum(i - nlat, 0), 0)),
                  pl.BlockSpec((rb, CONV_C), row),
                  pl.BlockSpec((rb, 2 * D_MODEL), row),
                  *_token_specs(rb, nlat, ctx_off),
                  pl.BlockSpec((1, N_MOD, D_MODEL), mod_idx),
                  pl.BlockSpec(wao.shape, const),
                  pl.BlockSpec(wco.shape, const),
                  pl.BlockSpec(wout.shape, const),
                  pl.BlockSpec((1, D_MODEL), const),
                  pl.BlockSpec((1, D_MODEL), const),
                  pl.BlockSpec(wr_hi.shape, const),
                  pl.BlockSpec(wr_lo.shape, const),
                  pl.BlockSpec((N_EXPERTS, 1), const)],
        out_specs=[pl.BlockSpec((rb, D_MODEL), row),
                   pl.BlockSpec((rb, D_MODEL + META_W), row),
                   pl.BlockSpec((8, rb), lambda i: (0, i)),
                   pl.BlockSpec((CLASS_ROWS, LANES), const)],
        out_shape=[jax.ShapeDtypeStruct((t, D_MODEL), F32),
                   jax.ShapeDtypeStruct((t, D_MODEL + META_W), F32),
                   jax.ShapeDtypeStruct((8, t), F32),
                   jax.ShapeDtypeStruct((CLASS_ROWS, LANES), F32)],
        scratch_shapes=[pltpu.VMEM((CLASS_ROWS, LANES), F32)],
        compiler_params=_params(("arbitrary",)),
        name="merge",
    )(o, o_ctx, z, gates, x_lat, x_ctx, mod, wao, wco, wout, lng, lnb, wr_hi, wr_lo, br)


def _dispatch_kernel(pos_ref, h_ref, init_ref, out_ref, buf_ref, load_sem, scat_sem, *, rb, n_steps):
    del init_ref
    i = pl.program_id(0)
    slot = i % 2

    def load(step, s):
        return pltpu.make_async_copy(h_ref.at[pl.ds(step * rb, rb)], buf_ref.at[s], load_sem.at[s])

    def wait_scatter(s):
        pltpu.make_async_copy(buf_ref.at[s], out_ref.at[pl.ds(0, rb)], scat_sem.at[s]).wait()

    @pl.when(i == 0)
    def _():
        load(0, 0).start()

    @pl.when(i > 0)
    def _():
        wait_scatter(1 - slot)

    @pl.when(i + 1 < n_steps)
    def _():
        load(i + 1, 1 - slot).start()

    load(i, slot).wait()

    def issue(r, carry):
        pltpu.make_async_copy(buf_ref.at[slot, pl.ds(r, 1)], out_ref.at[pl.ds(pos_ref[0, 0, r], 1)],
                              scat_sem.at[slot]).start()
        return carry

    lax.fori_loop(0, rb, issue, 0, unroll=8)

    @pl.when(i == n_steps - 1)
    def _():
        wait_scatter(slot)


def _dispatch(pos3, hext, init, *, rb, n_rows):
    n_steps = n_rows // rb
    kern = functools.partial(_dispatch_kernel, rb=rb, n_steps=n_steps)
    return pl.pallas_call(
        kern,
        grid=(n_steps,),
        in_specs=[pl.BlockSpec((1, 1, rb), lambda i: (i, 0, 0), memory_space=pltpu.SMEM),
                  pl.BlockSpec(memory_space=pl.ANY),
                  pl.BlockSpec(memory_space=pl.ANY)],
        out_specs=pl.BlockSpec(memory_space=pl.ANY),
        out_shape=jax.ShapeDtypeStruct(init.shape, F32),
        scratch_shapes=[pltpu.VMEM((2, rb, hext.shape[1]), F32), pltpu.SemaphoreType.DMA((2,)),
                        pltpu.SemaphoreType.DMA((2,))],
        input_output_aliases={2: 0},
        compiler_params=_params(("arbitrary",)),
        name="dispatch",
    )(pos3, hext, init)


def _experts_kernel(elo_ref, ehi_ref, valid_ref, x_ref, wgu_lo_ref, wgu_hi_ref, wd_lo_ref, wd_hi_ref, y_ref):
    j = pl.program_id(0)

    @pl.when(valid_ref[j] > 0)
    def _():
        xb = x_ref[:, :D_MODEL].astype(BF16)
        wts = x_ref[:, D_MODEL:]

        def expert(wgu_ref, wd_ref):
            gu = _dot(xb, wgu_ref[0])
            gate = gu[:, :D_EXPERT]
            act = gate * _sigmoid(gate) * gu[:, D_EXPERT:]
            return _dot(act.astype(BF16), wd_ref[0])

        y_ref[...] = (wts[:, 0:1] * expert(wgu_lo_ref, wd_lo_ref)
                      + wts[:, 1:2] * expert(wgu_hi_ref, wd_hi_ref))

    @pl.when(valid_ref[j] == 0)
    def _():
        y_ref[...] = jnp.zeros(y_ref.shape, F32)


def _experts(tile_lo, tile_hi, tile_valid, hs, wgu, wd, *, tm):
    p, width = hs.shape
    grid_spec = pltpu.PrefetchScalarGridSpec(
        num_scalar_prefetch=3,
        grid=(p // tm,),
        in_specs=[pl.BlockSpec((tm, width), lambda j, lo, hi, v: (j, 0)),
                  pl.BlockSpec((1, D_MODEL, 2 * D_EXPERT), lambda j, lo, hi, v: (lo[j], 0, 0)),
                  pl.BlockSpec((1, D_MODEL, 2 * D_EXPERT), lambda j, lo, hi, v: (hi[j], 0, 0)),
                  pl.BlockSpec((1, D_EXPERT, D_MODEL), lambda j, lo, hi, v: (lo[j], 0, 0)),
                  pl.BlockSpec((1, D_EXPERT, D_MODEL), lambda j, lo, hi, v: (hi[j], 0, 0))],
        out_specs=pl.BlockSpec((tm, D_MODEL), lambda j, lo, hi, v: (j, 0)),
    )
    return pl.pallas_call(
        _experts_kernel,
        grid_spec=grid_spec,
        out_shape=jax.ShapeDtypeStruct((p, D_MODEL), F32),
        compiler_params=_params(("arbitrary",)),
        name="experts",
    )(tile_lo, tile_hi, tile_valid, hs, wgu, wgu, wd, wd)


def _combine_kernel(pos_ref, pos_next_ref, ys_ref, x1_ref, mod_ref, lng_ref, lnb_ref, out_ref, ybuf_ref, sem, *,
                    rb, alpha, n_steps):
    i = pl.program_id(0)
    slot = i % 2

    def gather(p_ref, dst_slot):
        def issue(r, carry):
            pltpu.make_async_copy(ys_ref.at[pl.ds(p_ref[0, 0, r], 1)], ybuf_ref.at[dst_slot, pl.ds(r, 1)],
                                  sem.at[dst_slot]).start()
            return carry
        lax.fori_loop(0, rb, issue, 0, unroll=8)

    @pl.when(i == 0)
    def _():
        gather(pos_ref, 0)

    @pl.when(i + 1 < n_steps)
    def _():
        gather(pos_next_ref, 1 - slot)

    pltpu.make_async_copy(ys_ref.at[pl.ds(0, rb)], ybuf_ref.at[slot], sem.at[slot]).wait()

    m = mod_ref[0]
    out_ref[...] = _layer_norm(alpha * x1_ref[...] + m[5:6] * ybuf_ref[slot], lng_ref[...], lnb_ref[...])


def _combine(pos3, ys, x1, mod, lng, lnb, *, rb, n_rows, n_lat, seq, batch, alpha):
    nlat = n_lat // rb

    def mod_idx(i):
        return (jnp.where(i < nlat, (i * rb) // seq, batch), 0, 0)

    const = lambda i: (0, 0)
    n_steps = n_rows // rb
    kern = functools.partial(_combine_kernel, rb=rb, alpha=alpha, n_steps=n_steps)
    return pl.pallas_call(
        kern,
        grid=(n_steps,),
        in_specs=[pl.BlockSpec((1, 1, rb), lambda i: (i, 0, 0), memory_space=pltpu.SMEM),
                  pl.BlockSpec((1, 1, rb), lambda i: (jnp.minimum(i + 1, n_steps - 1), 0, 0),
                               memory_space=pltpu.SMEM),
                  pl.BlockSpec(memory_space=pl.ANY),
                  pl.BlockSpec((rb, D_MODEL), lambda i: (i, 0)),
                  pl.BlockSpec((1, N_MOD, D_MODEL), mod_idx),
                  pl.BlockSpec((1, D_MODEL), const),
                  pl.BlockSpec((1, D_MODEL), const)],
        out_specs=pl.BlockSpec((rb, D_MODEL), lambda i: (i, 0)),
        out_shape=jax.ShapeDtypeStruct((n_rows, D_MODEL), F32),
        scratch_shapes=[pltpu.VMEM((2, rb, D_MODEL), F32), pltpu.SemaphoreType.DMA((2,))],
        compiler_params=_params(("arbitrary",)),
        name="combine",
    )(pos3, pos3, ys, x1, mod, lng, lnb)


def _rope_tables(seq, pad_rows):
    rows = seq // GRID_W
    row = jnp.repeat(jnp.arange(rows, dtype=F32), GRID_W)
    col = jnp.tile(jnp.arange(GRID_W, dtype=F32), rows)
    inv_freq = ROPE_BASE ** (-jnp.arange(0, ROPE_AXIS_DIM, 2, dtype=F32) / ROPE_AXIS_DIM)
    ang = jnp.concatenate([row[:, None] * inv_freq, col[:, None] * inv_freq], -1)
    cos, sin = jnp.cos(ang), jnp.sin(ang)
    cos64 = jnp.concatenate([cos, cos], -1)
    sin64 = jnp.concatenate([-sin, sin], -1)
    cos_t = jnp.concatenate([cos64, cos64], -1)
    sin_t = jnp.concatenate([sin64, sin64], -1)
    cos_t = jnp.concatenate([cos_t, jnp.ones((pad_rows, LANES), F32)], 0)
    sin_t = jnp.concatenate([sin_t, jnp.zeros((pad_rows, LANES), F32)], 0)
    return cos_t, sin_t


def _class_tables():
    lo, hi = [], []
    for g in range(N_GROUPS):
        for a in range(EXPERTS_PER_GROUP):
            for b in range(a + 1, EXPERTS_PER_GROUP):
                lo.append(g * EXPERTS_PER_GROUP + a)
                hi.append(g * EXPERTS_PER_GROUP + b)
    return jnp.array(lo, jnp.int32), jnp.array(hi, jnp.int32)


def kernel(x, c, ctx, c_ctx, w_ada, b_ada, w_in, lam_q1, lam_k1, lam_q2, lam_k2, subln_g, w_attn_o, conv_w, conv_b, conv_ln_g, conv_ln_b, w_conv_o, w_out, ln1_g, ln1_b, w_router, b_router, w_e_gate, w_e_up, w_e_down, ln2_g, ln2_b):
    batch, seq, d = x.shape
    n_ctx = ctx.shape[1]
    depth = w_ada.shape[0]
    n_lat = batch * seq
    t = n_lat + batch * n_ctx
    alpha = (2 * depth) ** 0.25

    rb = 512 if (seq % 512 == 0 and (batch * n_ctx) % 512 == 0) else 256
    rbd = 256
    tm = 256
    qb = 1024 if seq % 1024 == 0 else 256
    kc = 512 if seq % 512 == 0 else 256
    assert batch + 1 <= 8 and seq % rb == 0 and seq % n_ctx == 0 and n_ctx % CONV_HALO == 0

    q_end, k_end, v_end = QK_W, 2 * QK_W, 3 * QK_W
    u_end = v_end + 2 * CONV_C
    perm64 = jnp.concatenate([jnp.arange(0, HEAD_DIM, 2), jnp.arange(1, HEAD_DIM, 2)])
    perm_qk = (jnp.arange(2 * QK_W // HEAD_DIM)[:, None] * HEAD_DIM + perm64[None, :]).reshape(-1)

    cos_t, sin_t = _rope_tables(seq, rb)
    cls_lo, cls_hi = _class_tables()
    p_rows = ((t + tm - 1) // tm + N_CLASSES) * tm

    cs = jnp.zeros((8, d), F32).at[:batch].set(c).at[batch].set(c_ctx)
    stream = (x.reshape(n_lat, d), ctx.reshape(batch * n_ctx, d), 0)
    wr_hi, wr_lo = _split_bf16(w_router.T)
    br = b_router.reshape(N_EXPERTS, 1)

    for l in range(depth):
        last = l == depth - 1
        lam_init = 0.8 - 0.6 * math.exp(-0.3 * l)
        n_rows = n_lat if last else t

        mod = _ada(cs, w_ada[l], b_ada[l].reshape(1, -1)).reshape(8, N_MOD, d)

        w_l = w_in[l]
        wqk = w_l[:, :k_end][:, perm_qk].astype(BF16)
        wvt = w_l[:, k_end:v_end].T.astype(BF16)
        wu = w_l[:, v_end:u_end].astype(BF16)
        wg = w_l[:, u_end:].astype(BF16)
        q, k1, k2, vt, y, gates = _inproj(*stream, mod, cos_t, sin_t, wqk, wvt, wu, wg,
                                          rb=rb, n_rows=t, n_lat=n_lat, seq=seq, batch=batch)

        lamv = jnp.stack([lam_q1[l], lam_k1[l], lam_q2[l], lam_k2[l]])
        g_col = subln_g[l].reshape(HEAD_W, 1)
        o = _attn_latent(q, k1, k2, vt, lamv, g_col, lam_init=lam_init, batch=batch, seq=seq,
                         ctx=n_ctx, qb=qb, kc=kc)
        o_ctx = o if last else _attn_ctx(q, k1, k2, vt, lamv, g_col, lam_init=lam_init, batch=batch, seq=seq,
                                         ctx=n_ctx)

        cw = jnp.concatenate([conv_w[l], jnp.zeros((1, CONV_C), F32)], 0)
        z = _conv(y, cw, conv_b[l].reshape(1, -1), conv_ln_g[l].reshape(1, -1), conv_ln_b[l].reshape(1, -1),
                  n_rows=n_rows, n_lat=n_lat, seq=seq, ctx=n_ctx)

        x1, hext, meta, counts = _merge(
            o, o_ctx, z, gates, *stream, mod, w_attn_o[l].astype(BF16), w_conv_o[l].astype(BF16), w_out[l].astype(BF16),
            ln1_g[l].reshape(1, -1), ln1_b[l].reshape(1, -1), wr_hi, wr_lo, br,
            rb=rb, n_rows=n_rows, n_lat=n_lat, seq=seq, batch=batch, alpha=alpha)

        cnt = counts[:N_CLASSES, 0].astype(jnp.int32)
        padded = ((cnt + tm - 1) // tm) * tm
        ends = jnp.cumsum(padded)
        starts = ends - padded
        cls = meta[0, :n_rows].astype(jnp.int32)
        pos = starts[cls] + meta[1, :n_rows].astype(jnp.int32)
        pos3 = pos.reshape(n_rows // rbd, 1, rbd)
        tile_row0 = jnp.arange(p_rows // tm, dtype=jnp.int32) * tm
        tile_cls = jnp.minimum(jnp.sum(tile_row0[:, None] >= ends[None, :], axis=1), N_CLASSES - 1)
        tile_valid = (tile_row0 < ends[-1]).astype(jnp.int32)
        tile_lo, tile_hi = cls_lo[tile_cls], cls_hi[tile_cls]

        hs = _dispatch(pos3, hext, jnp.zeros((p_rows, d + META_W), F32), rb=rbd, n_rows=n_rows)
        wgu = jnp.concatenate([w_e_gate[l], w_e_up[l]], -1).astype(BF16)
        ys = _experts(tile_lo, tile_hi, tile_valid, hs, wgu, w_e_down[l].astype(BF16), tm=tm)
        x_all = _combine(pos3, ys, x1, mod, ln2_g[l].reshape(1, -1), ln2_b[l].reshape(1, -1),
                         rb=rbd, n_rows=n_rows, n_lat=n_lat, seq=seq, batch=batch, alpha=alpha)
        stream = (x_all, x_all, n_lat // rb)

    return x_all.reshape(batch, seq, d)
```

```python
import functools
import math

import jax
import jax.numpy as jnp
from jax import lax
from jax.experimental import pallas as pl
from jax.experimental.pallas import tpu as pltpu

F32 = jnp.float32
BF16 = jnp.bfloat16

D_MODEL = 1024
HEADS = 4
HEAD_DIM = 64
HEAD_W = 2 * HEAD_DIM
QK_W = HEADS * HEAD_W
CONV_C = 512
CONV_K = 31
CONV_HALO = 16
N_EXPERTS = 16
N_GROUPS = 4
EXPERTS_PER_GROUP = 4
PAIRS_PER_GROUP = 6
N_CLASSES = N_GROUPS * PAIRS_PER_GROUP
CLASS_ROWS = 32
D_EXPERT = 512
N_MOD = 6
GRID_W = 64
ROPE_AXIS_DIM = HEAD_DIM // 2
ROPE_BASE = 10000.0
EPS = 1e-5
LANES = 128
SUBLANES = 8
META_W = LANES
NEG_BIG = -1e30

VMEM_LIMIT = 56 * 1024 * 1024


def _dot(a, b):
    return jnp.dot(a, b, preferred_element_type=F32)


def _dot_nt(a, b):
    return lax.dot_general(a, b, (((1,), (1,)), ((), ())), preferred_element_type=F32)


def _split_bf16(a):
    hi = a.astype(BF16)
    lo = (a - hi.astype(F32)).astype(BF16)
    return hi, lo


def _sigmoid(x):
    return 1.0 / (1.0 + jnp.exp(-x))


def _layer_norm(x, g, b):
    mu = jnp.mean(x, axis=-1, keepdims=True)
    xc = x - mu
    var = jnp.mean(xc * xc, axis=-1, keepdims=True)
    return xc * lax.rsqrt(var + EPS) * g + b


def _params(sem):
    return pltpu.CompilerParams(dimension_semantics=sem, vmem_limit_bytes=VMEM_LIMIT)


def _ada_kernel(c_ref, w_ref, b_ref, o_ref):
    c = c_ref[...]
    a = c * _sigmoid(c)
    a_hi, a_lo = _split_bf16(a)
    w_hi, w_lo = _split_bf16(w_ref[...])
    o_ref[...] = _dot(a_hi, w_hi) + _dot(a_lo, w_hi) + _dot(a_hi, w_lo) + b_ref[...]


def _ada(cs, w, b):
    n = w.shape[1]
    nb = 1536
    return pl.pallas_call(
        _ada_kernel,
        grid=(n // nb,),
        in_specs=[pl.BlockSpec((8, D_MODEL), lambda j: (0, 0)),
                  pl.BlockSpec((D_MODEL, nb), lambda j: (0, j)),
                  pl.BlockSpec((1, nb), lambda j: (0, j))],
        out_specs=pl.BlockSpec((8, nb), lambda j: (0, j)),
        out_shape=jax.ShapeDtypeStruct((8, n), F32),
        compiler_params=_params(("arbitrary",)),
        name="ada",
    )(cs, w, b)


def _inproj_kernel(x_ref, xc_ref, mod_ref, cos_ref, sin_ref, wqk_ref, wvt_ref, wu_ref, wg_ref,
                   q_ref, k1_ref, k2_ref, vt_ref, y_ref, g_ref, *, nlat):
    m = mod_ref[0]
    x = jnp.where(pl.program_id(0) < nlat, x_ref[...], xc_ref[...])
    hx = (x * (1.0 + m[1:2]) + m[0:1]).astype(BF16)

    qk = _dot(hx, wqk_ref[...])
    w = qk.shape[1]
    lane = lax.broadcasted_iota(jnp.int32, qk.shape, 1)
    first_half = (lane % HEAD_DIM) < (HEAD_DIM // 2)
    partner = jnp.where(first_half,
                        pltpu.roll(qk, w - HEAD_DIM // 2, 1),
                        pltpu.roll(qk, HEAD_DIM // 2, 1))
    reps = w // LANES
    cos = jnp.concatenate([cos_ref[...]] * reps, axis=1)
    sin = jnp.concatenate([sin_ref[...]] * reps, axis=1)
    roped = qk * cos + partner * sin
    q_ref[...] = (roped[:, :QK_W] * (HEAD_DIM ** -0.5 * LOG2E)).astype(BF16)
    k = roped[:, QK_W:]
    map1 = (lax.broadcasted_iota(jnp.int32, k.shape, 1) % HEAD_W) < HEAD_DIM
    k1_ref[...] = jnp.where(map1, k, 0.0).astype(BF16)
    k2_ref[...] = jnp.where(map1, 0.0, k).astype(BF16)

    vt_ref[...] = _dot_nt(wvt_ref[...], hx).astype(BF16)

    u = _dot(hx, wu_ref[...])
    y_ref[...] = u[:, :CONV_C] * _sigmoid(u[:, CONV_C:])
    g_ref[...] = _sigmoid(_dot(hx, wg_ref[...])).astype(BF16)


def _token_specs(rb, nlat, ctx_off):
    return (pl.BlockSpec((rb, D_MODEL), lambda i: (jnp.minimum(i, nlat - 1), 0)),
            pl.BlockSpec((rb, D_MODEL), lambda i: (ctx_off + jnp.maximum(i - nlat, 0), 0)))


def _inproj(x_lat, x_ctx, ctx_off, mod, cos_t, sin_t, wqk, wvt, wu, wg, *, rb, n_rows, n_lat, seq, batch):
    t = n_rows
    nlat = n_lat // rb
    per_seq = seq // rb

    def mod_idx(i):
        return (jnp.where(i < nlat, (i * rb) // seq, batch), 0, 0)

    def rope_idx(i):
        return (jnp.where(i < nlat, i % per_seq, per_seq), 0)

    const = lambda i: (0, 0)
    row = lambda i: (i, 0)
    return pl.pallas_call(
        functools.partial(_inproj_kernel, nlat=nlat),
        grid=(t // rb,),
        in_specs=[*_token_specs(rb, nlat, ctx_off),
                  pl.BlockSpec((1, N_MOD, D_MODEL), mod_idx),
                  pl.BlockSpec((rb, LANES), rope_idx),
                  pl.BlockSpec((rb, LANES), rope_idx),
                  pl.BlockSpec(wqk.shape, const),
                  pl.BlockSpec(wvt.shape, const),
                  pl.BlockSpec(wu.shape, const),
                  pl.BlockSpec(wg.shape, const)],
        out_specs=[pl.BlockSpec((rb, QK_W), row),
                   pl.BlockSpec((rb, QK_W), row),
                   pl.BlockSpec((rb, QK_W), row),
                   pl.BlockSpec((QK_W, rb), lambda i: (0, i)),
                   pl.BlockSpec((rb, CONV_C), row),
                   pl.BlockSpec((rb, 2 * D_MODEL), row)],
        out_shape=[jax.ShapeDtypeStruct((t, QK_W), BF16),
                   jax.ShapeDtypeStruct((t, QK_W), BF16),
                   jax.ShapeDtypeStruct((t, QK_W), BF16),
                   jax.ShapeDtypeStruct((QK_W, t), BF16),
                   jax.ShapeDtypeStruct((t, CONV_C), F32),
                   jax.ShapeDtypeStruct((t, 2 * D_MODEL), BF16)],
        compiler_params=_params(("arbitrary",)),
        name="inproj",
    )(x_lat, x_ctx, mod, cos_t, sin_t, wqk, wvt, wu, wg)


ONES_ROWS = 16
ACC_ROWS = HEAD_W + ONES_ROWS
LOG2E = math.log2(math.e)


def _attn_kernel(*refs, lam_init, n_chunks, kc):
    if n_chunks:
        (q_ref, kc1_ref, kc2_ref, vtc_ref, k1_ref, k2_ref, vt_ref, lam_ref, g_ref,
         o_ref, acc1_ref, acc2_ref, sa1_ref, sa2_ref, sb1_ref, sb2_ref) = refs
        key_refs = (k1_ref, k2_ref)
    else:
        (q_ref, kc1_ref, kc2_ref, vtc_ref, lam_ref, g_ref, o_ref, acc1_ref, acc2_ref) = refs
    acc_refs = (acc1_ref, acc2_ref)
    q = q_ref[...]
    qb = q.shape[0]

    def with_ones(vt):
        return jnp.concatenate([vt, jnp.ones((ONES_ROWS, vt.shape[1]), BF16)], axis=0)

    def update(i, s, cmax, m_old, vt_ext, first=False):
        m_new = jnp.maximum(m_old, cmax)
        p = jnp.exp2(s - m_new).astype(BF16)
        pv = _dot(vt_ext, p)
        if first:
            acc_refs[i][...] = pv
        else:
            acc_refs[i][...] = jnp.exp2(m_old - m_new) * acc_refs[i][...] + pv
        return m_new

    vtc_ext = with_ones(vtc_ref[...])
    m_init = jnp.full((1, qb), NEG_BIG, F32)
    n_ctx = kc1_ref.shape[0]
    ctx_scores = []
    for i, kref in enumerate((kc1_ref, kc2_ref)):
        s = _dot_nt(kref[...], q)
        cmax = jnp.max(s, axis=0, keepdims=True)
        if n_chunks:
            (sb1_ref, sb2_ref)[i][0:n_ctx, :] = s
        ctx_scores.append((s, cmax))

    if not n_chunks:
        ms = [update(i, s, cmax, m_init, vtc_ext, first=True) for i, (s, cmax) in enumerate(ctx_scores)]

    if n_chunks:
        def scores(i, c, s_refs):
            off = pl.multiple_of(c * kc, kc)
            s = _dot_nt(key_refs[i][pl.ds(off, kc), :], q)
            s_refs[i][...] = s
            return jnp.max(s, axis=0, keepdims=True)

        def values(c):
            return with_ones(vt_ref[:, pl.ds(pl.multiple_of(c * kc, kc), kc)])

        def advance(c_next, s_next, c, s_cur, cm_cur, m):
            vt_ext = values(c)
            cm_next, m = [None, None], list(m)
            for i in range(2):
                cm_next[i] = scores(i, c_next, s_next)
                m[i] = update(i, s_cur[i][...], cm_cur[i], m[i], vt_ext)
            return cm_next, m

        sa = (sa1_ref, sa2_ref)
        sb = (sb1_ref, sb2_ref)
        cm_a, ms = [None, None], [None, None]
        for i, (_, cmax) in enumerate(ctx_scores):
            cm_a[i] = scores(i, 0, sa)
            ms[i] = update(i, sb[i][0:n_ctx, :], cmax, m_init, vtc_ext, first=True)

        def body(j, carry):
            c = 2 * j
            cm_b, m = advance(c + 1, sb, c, sa, carry[2:], carry[:2])
            cm_a, m = advance(c + 2, sa, c + 1, sb, cm_b, m)
            return (*m, *cm_a)

        carry = lax.fori_loop(0, n_chunks // 2 - 1, body, (*ms, *cm_a))
        cm_b, m = advance(n_chunks - 1, sb, n_chunks - 2, sa, carry[2:], carry[:2])
        vt_ext = values(n_chunks - 1)
        for i in range(2):
            update(i, sb[i][...], cm_b[i], m[i], vt_ext)

    lv = lam_ref[...]
    lam = (jnp.exp(jnp.sum(lv[0:1] * lv[1:2], axis=1, keepdims=True))
           - jnp.exp(jnp.sum(lv[2:3] * lv[3:4], axis=1, keepdims=True)) + lam_init)
    o1 = acc1_ref[0:HEAD_W, :] / acc1_ref[HEAD_W:HEAD_W + 1, :]
    o2 = acc2_ref[0:HEAD_W, :] / acc2_ref[HEAD_W:HEAD_W + 1, :]
    o_t = o1 - lam * o2
    mean_sq = jnp.mean(o_t * o_t, axis=0, keepdims=True)
    o_t = o_t * lax.rsqrt(mean_sq + EPS) * (g_ref[...] * (1.0 - lam_init))
    o_ref[...] = o_t.T.astype(BF16)


def _attn_latent(q, k1, k2, vt, lamv, g_col, *, lam_init, batch, seq, ctx, qb, kc):
    n_lat = batch * seq
    nq = seq // qb
    ctx0 = n_lat // ctx
    n_chunks = seq // kc
    assert n_chunks >= 2 and n_chunks % 2 == 0 and ctx <= kc
    kern = functools.partial(_attn_kernel, lam_init=lam_init, n_chunks=n_chunks, kc=kc)
    return pl.pallas_call(
        kern,
        grid=(batch, HEADS, nq),
        in_specs=[pl.BlockSpec((qb, HEAD_W), lambda b, h, i: (b * nq + i, h)),
                  pl.BlockSpec((ctx, HEAD_W), lambda b, h, i: (ctx0 + b, h)),
                  pl.BlockSpec((ctx, HEAD_W), lambda b, h, i: (ctx0 + b, h)),
                  pl.BlockSpec((HEAD_W, ctx), lambda b, h, i: (h, ctx0 + b)),
                  pl.BlockSpec((seq, HEAD_W), lambda b, h, i: (b, h)),
                  pl.BlockSpec((seq, HEAD_W), lambda b, h, i: (b, h)),
                  pl.BlockSpec((HEAD_W, seq), lambda b, h, i: (h, b)),
                  pl.BlockSpec((4, HEAD_DIM), lambda b, h, i: (0, 0)),
                  pl.BlockSpec((HEAD_W, 1), lambda b, h, i: (0, 0))],
        out_specs=pl.BlockSpec((qb, HEAD_W), lambda b, h, i: (b * nq + i, h)),
        out_shape=jax.ShapeDtypeStruct((n_lat, QK_W), BF16),
        scratch_shapes=[pltpu.VMEM((ACC_ROWS, qb), F32)] * 2 + [pltpu.VMEM((kc, qb), F32)] * 4,
        compiler_params=_params(("arbitrary", "arbitrary", "arbitrary")),
        name="attn_latent",
    )(q, k1, k2, vt, k1, k2, vt, lamv, g_col)


def _attn_ctx(q, k1, k2, vt, lamv, g_col, *, lam_init, batch, seq, ctx):
    ctx0 = batch * seq // ctx
    kern = functools.partial(_attn_kernel, lam_init=lam_init, n_chunks=0, kc=0)
    return pl.pallas_call(
        kern,
        grid=(batch, HEADS),
        in_specs=[pl.BlockSpec((ctx, HEAD_W), lambda b, h: (ctx0 + b, h)),
                  pl.BlockSpec((ctx, HEAD_W), lambda b, h: (ctx0 + b, h)),
                  pl.BlockSpec((ctx, HEAD_W), lambda b, h: (ctx0 + b, h)),
                  pl.BlockSpec((HEAD_W, ctx), lambda b, h: (h, ctx0 + b)),
                  pl.BlockSpec((4, HEAD_DIM), lambda b, h: (0, 0)),
                  pl.BlockSpec((HEAD_W, 1), lambda b, h: (0, 0))],
        out_specs=pl.BlockSpec((ctx, HEAD_W), lambda b, h: (b, h)),
        out_shape=jax.ShapeDtypeStruct((batch * ctx, QK_W), BF16),
        scratch_shapes=[pltpu.VMEM((ACC_ROWS, ctx), F32)] * 2,
        compiler_params=_params(("arbitrary", "arbitrary")),
        name="attn_ctx",
    )(q, k1, k2, vt, lamv, g_col)


CONV_ROW_TILE = 64


def _conv_kernel(y_ref, prev_ref, next_ref, w_ref, b_ref, g_ref, beta_ref, z_ref, buf_ref, shifted_ref, *,
                 rb, nlat, per_seq):
    i = pl.program_id(0)
    pos = i % per_seq
    first = jnp.logical_or(i >= nlat, pos == 0)
    last = jnp.logical_or(i >= nlat, pos == per_seq - 1)

    buf_ref[CONV_HALO:CONV_HALO + rb, :] = y_ref[...]

    @pl.when(first)
    def _():
        buf_ref[0:CONV_HALO, :] = jnp.zeros((CONV_HALO, CONV_C), F32)

    @pl.when(jnp.logical_not(first))
    def _():
        buf_ref[0:CONV_HALO, :] = prev_ref[...]

    @pl.when(last)
    def _():
        buf_ref[CONV_HALO + rb:, :] = jnp.zeros((CONV_HALO, CONV_C), F32)

    @pl.when(jnp.logical_not(last))
    def _():
        buf_ref[CONV_HALO + rb:, :] = next_ref[...]

    n_sh = shifted_ref.shape[1]
    for r in range(1, SUBLANES):
        shifted_ref[r] = buf_ref[r:r + n_sh, :]

    base = CONV_HALO - CONV_K // 2
    for r0 in range(0, rb, CONV_ROW_TILE):
        acc = jnp.zeros((CONV_ROW_TILE, CONV_C), F32) + b_ref[...]
        for k in range(CONV_K):
            a, r = divmod(base + k, SUBLANES)
            lo = r0 + a * SUBLANES
            rows = buf_ref[lo:lo + CONV_ROW_TILE, :] if r == 0 else shifted_ref[r, lo:lo + CONV_ROW_TILE, :]
            acc = acc + w_ref[k:k + 1, :] * rows
        v = _layer_norm(acc, g_ref[...], beta_ref[...])
        z_ref[r0:r0 + CONV_ROW_TILE, :] = (v * _sigmoid(v)).astype(BF16)


def _conv(y, w, b, g, beta, *, n_rows, n_lat, seq, ctx):
    rb = ctx
    hb = rb // CONV_HALO
    t = y.shape[0]
    n_halo = t // CONV_HALO
    kern = functools.partial(_conv_kernel, rb=rb, nlat=n_lat // rb, per_seq=seq // rb)
    const = lambda i: (0, 0)
    return pl.pallas_call(
        kern,
        grid=(n_rows // rb,),
        in_specs=[pl.BlockSpec((rb, CONV_C), lambda i: (i, 0)),
                  pl.BlockSpec((CONV_HALO, CONV_C), lambda i: (jnp.maximum(i * hb - 1, 0), 0)),
                  pl.BlockSpec((CONV_HALO, CONV_C), lambda i: (jnp.minimum((i + 1) * hb, n_halo - 1), 0)),
                  pl.BlockSpec(w.shape, const),
                  pl.BlockSpec((1, CONV_C), const),
                  pl.BlockSpec((1, CONV_C), const),
                  pl.BlockSpec((1, CONV_C), const)],
        out_specs=pl.BlockSpec((rb, CONV_C), lambda i: (i, 0)),
        out_shape=jax.ShapeDtypeStruct((n_rows, CONV_C), BF16),
        scratch_shapes=[pltpu.VMEM((rb + 2 * CONV_HALO, CONV_C), F32),
                        pltpu.VMEM((SUBLANES, rb + 2 * CONV_HALO - SUBLANES, CONV_C), F32)],
        compiler_params=_params(("arbitrary",)),
        name="conv",
    )(y, y, y, w, b, g, beta)


def _first_index(vals, target):
    idx = jnp.full(target.shape, len(vals) - 1, jnp.int32)
    for j in range(len(vals) - 2, -1, -1):
        idx = jnp.where(vals[j] == target, j, idx)
    return idx


def _route(logits_t):
    mx = jnp.max(logits_t, axis=0, keepdims=True)
    e = jnp.exp(logits_t - mx)
    m1s, m2s, i1s, i2s, scores = [], [], [], [], []
    for g in range(N_GROUPS):
        a = [e[g * EXPERTS_PER_GROUP + j:g * EXPERTS_PER_GROUP + j + 1, :] for j in range(EXPERTS_PER_GROUP)]
        m1 = functools.reduce(jnp.maximum, a)
        i1 = _first_index(a, m1)
        rest = [jnp.where(i1 == j, -1.0, a[j]) for j in range(EXPERTS_PER_GROUP)]
        m2 = functools.reduce(jnp.maximum, rest)
        i2 = _first_index(rest, m2)
        m1s.append(m1); m2s.append(m2); i1s.append(i1); i2s.append(i2); scores.append(m1 + m2)
    gstar = _first_index(scores, functools.reduce(jnp.maximum, scores))

    def pick(xs):
        out = xs[N_GROUPS - 1]
        for g in range(N_GROUPS - 2, -1, -1):
            out = jnp.where(gstar == g, xs[g], out)
        return out

    m1, m2, i1, i2 = pick(m1s), pick(m2s), pick(i1s), pick(i2s)
    tot = m1 + m2
    w1, w2 = m1 / tot, m2 / tot
    lo = jnp.minimum(i1, i2)
    hi = jnp.maximum(i1, i2)
    w_lo = jnp.where(i1 < i2, w1, w2)
    w_hi = jnp.where(i1 < i2, w2, w1)
    pair = jnp.where(lo == 0, 0, jnp.where(lo == 1, 3, 5)) + (hi - lo - 1)
    return gstar * PAIRS_PER_GROUP + pair, w_lo, w_hi


def _merge_kernel(o_ref, octx_ref, z_ref, g_ref, x_ref, xc_ref, mod_ref, wao_ref, wco_ref, wout_ref, lng_ref, lnb_ref,
                  wr_hi_ref, wr_lo_ref, br_ref,
                  x1_ref, hext_ref, meta_ref, cnt_ref, carry_ref, *, alpha, nlat):
    i = pl.program_id(0)

    @pl.when(i == 0)
    def _():
        carry_ref[...] = jnp.zeros(carry_ref.shape, F32)

    m = mod_ref[0]
    att = _dot(jnp.where(i < nlat, o_ref[...], octx_ref[...]), wao_ref[...])
    cnv = _dot(z_ref[...], wco_ref[...])
    gates = g_ref[...].astype(F32)
    mixin = gates[:, :D_MODEL] * att + gates[:, D_MODEL:] * cnv
    mix = _dot(mixin.astype(BF16), wout_ref[...])
    x = jnp.where(i < nlat, x_ref[...], xc_ref[...])
    x1 = _layer_norm(alpha * x + m[2:3] * mix, lng_ref[...], lnb_ref[...])
    x1_ref[...] = x1
    h = x1 * (1.0 + m[4:5]) + m[3:4]
    hext_ref[:, :D_MODEL] = h

    h_hi, h_lo = _split_bf16(h)
    logits_t = (_dot_nt(wr_hi_ref[...], h_hi) + _dot_nt(wr_hi_ref[...], h_lo)
                + _dot_nt(wr_lo_ref[...], h_hi) + br_ref[...])
    cls, w_lo, w_hi = _route(logits_t)
    rb = h.shape[0]

    crow = lax.broadcasted_iota(jnp.int32, (CLASS_ROWS, rb), 0)
    onehot = (crow == cls).astype(F32)
    tri = (lax.broadcasted_iota(jnp.int32, (rb, rb), 0)
           < lax.broadcasted_iota(jnp.int32, (rb, rb), 1)).astype(BF16)
    before = _dot(onehot.astype(BF16), tri) + carry_ref[:, 0:1]
    rank = jnp.sum(onehot * before, axis=0, keepdims=True)
    carry_ref[...] = carry_ref[...] + jnp.sum(onehot, axis=1, keepdims=True)
    cnt_ref[...] = carry_ref[...]

    mrow = lax.broadcasted_iota(jnp.int32, (8, rb), 0)
    meta_ref[...] = jnp.where(mrow == 0, cls.astype(F32), jnp.where(mrow == 1, rank, 0.0))

    wrow = lax.broadcasted_iota(jnp.int32, (META_W, rb), 0)
    wts_t = jnp.where(wrow == 0, w_lo, jnp.where(wrow == 1, w_hi, 0.0))
    hext_ref[:, D_MODEL:] = wts_t.T


def _merge(o, o_ctx, z, gates, x_lat, x_ctx, ctx_off, mod, wao, wco, wout, lng, lnb, wr_hi, wr_lo, br, *,
           rb, n_rows, n_lat, seq, batch, alpha):
    t = n_rows
    nlat = n_lat // rb

    def mod_idx(i):
        return (jnp.where(i < nlat, (i * rb) // seq, batch), 0, 0)

    const = lambda i: (0, 0)
    row = lambda i: (i, 0)
    kern = functools.partial(_merge_kernel, alpha=alpha, nlat=nlat)
    return pl.pallas_call(
        kern,
        grid=(n_rows // rb,),
        in_specs=[pl.BlockSpec((rb, QK_W), lambda i: (jnp.minimum(i, nlat - 1), 0)),
                  pl.BlockSpec((rb, QK_W), lambda i: (jnp.maximum(i - nlat, 0), 0)),
                  pl.BlockSpec((rb, CONV_C), row),
                  pl.BlockSpec((rb, 2 * D_MODEL), row),
                  *_token_specs(rb, nlat, ctx_off),
                  pl.BlockSpec((1, N_MOD, D_MODEL), mod_idx),
                  pl.BlockSpec(wao.shape, const),
                  pl.BlockSpec(wco.shape, const),
                  pl.BlockSpec(wout.shape, const),
                  pl.BlockSpec((1, D_MODEL), const),
                  pl.BlockSpec((1, D_MODEL), const),
                  pl.BlockSpec(wr_hi.shape, const),
                  pl.BlockSpec(wr_lo.shape, const),
                  pl.BlockSpec((N_EXPERTS, 1), const)],
        out_specs=[pl.BlockSpec((rb, D_MODEL), row),
                   pl.BlockSpec((rb, D_MODEL + META_W), row),
                   pl.BlockSpec((8, rb), lambda i: (0, i)),
                   pl.BlockSpec((CLASS_ROWS, LANES), const)],
        out_shape=[jax.ShapeDtypeStruct((t, D_MODEL), F32),
                   jax.ShapeDtypeStruct((t, D_MODEL + META_W), F32),
                   jax.ShapeDtypeStruct((8, t), F32),
                   jax.ShapeDtypeStruct((CLASS_ROWS, LANES), F32)],
        scratch_shapes=[pltpu.VMEM((CLASS_ROWS, LANES), F32)],
        compiler_params=_params(("arbitrary",)),
        name="merge",
    )(o, o_ctx, z, gates, x_lat, x_ctx, mod, wao, wco, wout, lng, lnb, wr_hi, wr_lo, br)


def _dispatch_kernel(pos_ref, h_ref, init_ref, out_ref, buf_ref, load_sem, scat_sem, *, rb, n_steps):
    del init_ref
    i = pl.program_id(0)
    slot = i % 2

    def load(step, s):
        return pltpu.make_async_copy(h_ref.at[pl.ds(step * rb, rb)], buf_ref.at[s], load_sem.at[s])

    def wait_scatter(s):
        pltpu.make_async_copy(buf_ref.at[s], out_ref.at[pl.ds(0, rb)], scat_sem.at[s]).wait()

    @pl.when(i == 0)
    def _():
        load(0, 0).start()

    @pl.when(i > 0)
    def _():
        wait_scatter(1 - slot)

    @pl.when(i + 1 < n_steps)
    def _():
        load(i + 1, 1 - slot).start()

    load(i, slot).wait()

    def issue(r, carry):
        pltpu.make_async_copy(buf_ref.at[slot, pl.ds(r, 1)], out_ref.at[pl.ds(pos_ref[0, 0, r], 1)],
                              scat_sem.at[slot]).start()
        return carry

    lax.fori_loop(0, rb, issue, 0, unroll=8)

    @pl.when(i == n_steps - 1)
    def _():
        wait_scatter(slot)


def _dispatch(pos3, hext, init, *, rb, n_rows):
    n_steps = n_rows // rb
    kern = functools.partial(_dispatch_kernel, rb=rb, n_steps=n_steps)
    return pl.pallas_call(
        kern,
        grid=(n_steps,),
        in_specs=[pl.BlockSpec((1, 1, rb), lambda i: (i, 0, 0), memory_space=pltpu.SMEM),
                  pl.BlockSpec(memory_space=pl.ANY),
                  pl.BlockSpec(memory_space=pl.ANY)],
        out_specs=pl.BlockSpec(memory_space=pl.ANY),
        out_shape=jax.ShapeDtypeStruct(init.shape, F32),
        scratch_shapes=[pltpu.VMEM((2, rb, hext.shape[1]), F32), pltpu.SemaphoreType.DMA((2,)),
                        pltpu.SemaphoreType.DMA((2,))],
        input_output_aliases={2: 0},
        compiler_params=_params(("arbitrary",)),
        name="dispatch",
    )(pos3, hext, init)


def _experts_kernel(elo_ref, ehi_ref, valid_ref, x_ref, wgu_lo_ref, wgu_hi_ref, wd_lo_ref, wd_hi_ref, y_ref):
    j = pl.program_id(0)

    @pl.when(valid_ref[j] > 0)
    def _():
        xb = x_ref[:, :D_MODEL].astype(BF16)
        wts = x_ref[:, D_MODEL:]

        def expert(wgu_ref, wd_ref):
            gu = _dot(xb, wgu_ref[0])
            gate = gu[:, :D_EXPERT]
            act = gate * _sigmoid(gate) * gu[:, D_EXPERT:]
            return _dot(act.astype(BF16), wd_ref[0])

        y_ref[...] = (wts[:, 0:1] * expert(wgu_lo_ref, wd_lo_ref)
                      + wts[:, 1:2] * expert(wgu_hi_ref, wd_hi_ref))

    @pl.when(valid_ref[j] == 0)
    def _():
        y_ref[...] = jnp.zeros(y_ref.shape, F32)


def _experts(tile_lo, tile_hi, tile_valid, hs, wgu, wd, *, tm):
    p, width = hs.shape
    grid_spec = pltpu.PrefetchScalarGridSpec(
        num_scalar_prefetch=3,
        grid=(p // tm,),
        in_specs=[pl.BlockSpec((tm, width), lambda j, lo, hi, v: (j, 0)),
                  pl.BlockSpec((1, D_MODEL, 2 * D_EXPERT), lambda j, lo, hi, v: (lo[j], 0, 0)),
                  pl.BlockSpec((1, D_MODEL, 2 * D_EXPERT), lambda j, lo, hi, v: (hi[j], 0, 0)),
                  pl.BlockSpec((1, D_EXPERT, D_MODEL), lambda j, lo, hi, v: (lo[j], 0, 0)),
                  pl.BlockSpec((1, D_EXPERT, D_MODEL), lambda j, lo, hi, v: (hi[j], 0, 0))],
        out_specs=pl.BlockSpec((tm, D_MODEL), lambda j, lo, hi, v: (j, 0)),
    )
    return pl.pallas_call(
        _experts_kernel,
        grid_spec=grid_spec,
        out_shape=jax.ShapeDtypeStruct((p, D_MODEL), F32),
        compiler_params=_params(("arbitrary",)),
        name="experts",
    )(tile_lo, tile_hi, tile_valid, hs, wgu, wgu, wd, wd)


def _combine_kernel(pos_ref, pos_next_ref, ys_ref, x1_ref, mod_ref, lng_ref, lnb_ref, out_ref, ybuf_ref, sem, *,
                    rb, alpha, n_steps):
    i = pl.program_id(0)
    slot = i % 2

    def gather(p_ref, dst_slot):
        def issue(r, carry):
            pltpu.make_async_copy(ys_ref.at[pl.ds(p_ref[0, 0, r], 1)], ybuf_ref.at[dst_slot, pl.ds(r, 1)],
                                  sem.at[dst_slot]).start()
            return carry
        lax.fori_loop(0, rb, issue, 0, unroll=8)

    @pl.when(i == 0)
    def _():
        gather(pos_ref, 0)

    @pl.when(i + 1 < n_steps)
    def _():
        gather(pos_next_ref, 1 - slot)

    pltpu.make_async_copy(ys_ref.at[pl.ds(0, rb)], ybuf_ref.at[slot], sem.at[slot]).wait()

    m = mod_ref[0]
    out_ref[...] = _layer_norm(alpha * x1_ref[...] + m[5:6] * ybuf_ref[slot], lng_ref[...], lnb_ref[...])


def _combine(pos3, ys, x1, mod, lng, lnb, *, rb, n_rows, n_lat, seq, batch, alpha):
    nlat = n_lat // rb

    def mod_idx(i):
        return (jnp.where(i < nlat, (i * rb) // seq, batch), 0, 0)

    const = lambda i: (0, 0)
    n_steps = n_rows // rb
    kern = functools.partial(_combine_kernel, rb=rb, alpha=alpha, n_steps=n_steps)
    return pl.pallas_call(
        kern,
        grid=(n_steps,),
        in_specs=[pl.BlockSpec((1, 1, rb), lambda i: (i, 0, 0), memory_space=pltpu.SMEM),
                  pl.BlockSpec((1, 1, rb), lambda i: (jnp.minimum(i + 1, n_steps - 1), 0, 0),
                               memory_space=pltpu.SMEM),
                  pl.BlockSpec(memory_space=pl.ANY),
                  pl.BlockSpec((rb, D_MODEL), lambda i: (i, 0)),
                  pl.BlockSpec((1, N_MOD, D_MODEL), mod_idx),
                  pl.BlockSpec((1, D_MODEL), const),
                  pl.BlockSpec((1, D_MODEL), const)],
        out_specs=pl.BlockSpec((rb, D_MODEL), lambda i: (i, 0)),
        out_shape=jax.ShapeDtypeStruct((n_rows, D_MODEL), F32),
        scratch_shapes=[pltpu.VMEM((2, rb, D_MODEL), F32), pltpu.SemaphoreType.DMA((2,))],
        compiler_params=_params(("arbitrary",)),
        name="combine",
    )(pos3, pos3, ys, x1, mod, lng, lnb)


def _rope_tables(seq, pad_rows):
    rows = seq // GRID_W
    row = jnp.repeat(jnp.arange(rows, dtype=F32), GRID_W)
    col = jnp.tile(jnp.arange(GRID_W, dtype=F32), rows)
    inv_freq = ROPE_BASE ** (-jnp.arange(0, ROPE_AXIS_DIM, 2, dtype=F32) / ROPE_AXIS_DIM)
    ang = jnp.concatenate([row[:, None] * inv_freq, col[:, None] * inv_freq], -1)
    cos, sin = jnp.cos(ang), jnp.sin(ang)
    cos64 = jnp.concatenate([cos, cos], -1)
    sin64 = jnp.concatenate([-sin, sin], -1)
    cos_t = jnp.concatenate([cos64, cos64], -1)
    sin_t = jnp.concatenate([sin64, sin64], -1)
    cos_t = jnp.concatenate([cos_t, jnp.ones((pad_rows, LANES), F32)], 0)
    sin_t = jnp.concatenate([sin_t, jnp.zeros((pad_rows, LANES), F32)], 0)
    return cos_t, sin_t


def _class_tables():
    lo, hi = [], []
    for g in range(N_GROUPS):
        for a in range(EXPERTS_PER_GROUP):
            for b in range(a + 1, EXPERTS_PER_GROUP):
                lo.append(g * EXPERTS_PER_GROUP + a)
                hi.append(g * EXPERTS_PER_GROUP + b)
    return jnp.array(lo, jnp.int32), jnp.array(hi, jnp.int32)


def kernel(x, c, ctx, c_ctx, w_ada, b_ada, w_in, lam_q1, lam_k1, lam_q2, lam_k2, subln_g, w_attn_o, conv_w, conv_b, conv_ln_g, conv_ln_b, w_conv_o, w_out, ln1_g, ln1_b, w_router, b_router, w_e_gate, w_e_up, w_e_down, ln2_g, ln2_b):
    batch, seq, d = x.shape
    n_ctx = ctx.shape[1]
    depth = w_ada.shape[0]
    n_lat = batch * seq
    t = n_lat + batch * n_ctx
    alpha = (2 * depth) ** 0.25

    rb = 512 if (seq % 512 == 0 and (batch * n_ctx) % 512 == 0) else 256
    rbd = 256
    tm = 256
    qb = 1024 if seq % 1024 == 0 else 256
    kc = 512 if seq % 512 == 0 else 256
    assert batch + 1 <= 8 and seq % rb == 0 and seq % n_ctx == 0 and n_ctx % CONV_HALO == 0

    q_end, k_end, v_end = QK_W, 2 * QK_W, 3 * QK_W
    u_end = v_end + 2 * CONV_C
    perm64 = jnp.concatenate([jnp.arange(0, HEAD_DIM, 2), jnp.arange(1, HEAD_DIM, 2)])
    perm_qk = (jnp.arange(2 * QK_W // HEAD_DIM)[:, None] * HEAD_DIM + perm64[None, :]).reshape(-1)

    cos_t, sin_t = _rope_tables(seq, rb)
    cls_lo, cls_hi = _class_tables()
    p_rows = ((t + tm - 1) // tm + N_CLASSES) * tm

    cs = jnp.zeros((8, d), F32).at[:batch].set(c).at[batch].set(c_ctx)
    stream = (x.reshape(n_lat, d), ctx.reshape(batch * n_ctx, d), 0)
    wr_hi, wr_lo = _split_bf16(w_router.T)
    br = b_router.reshape(N_EXPERTS, 1)

    for l in range(depth):
        last = l == depth - 1
        lam_init = 0.8 - 0.6 * math.exp(-0.3 * l)
        n_rows = n_lat if last else t

        mod = _ada(cs, w_ada[l], b_ada[l].reshape(1, -1)).reshape(8, N_MOD, d)

        w_l = w_in[l]
        wqk = w_l[:, :k_end][:, perm_qk].astype(BF16)
        wvt = w_l[:, k_end:v_end].T.astype(BF16)
        wu = w_l[:, v_end:u_end].astype(BF16)
        wg = w_l[:, u_end:].astype(BF16)
        q, k1, k2, vt, y, gates = _inproj(*stream, mod, cos_t, sin_t, wqk, wvt, wu, wg,
                                          rb=rb, n_rows=t, n_lat=n_lat, seq=seq, batch=batch)

        lamv = jnp.stack([lam_q1[l], lam_k1[l], lam_q2[l], lam_k2[l]])
        g_col = subln_g[l].reshape(HEAD_W, 1)
        o = _attn_latent(q, k1, k2, vt, lamv, g_col, lam_init=lam_init, batch=batch, seq=seq,
                         ctx=n_ctx, qb=qb, kc=kc)
        o_ctx = o if last else _attn_ctx(q, k1, k2, vt, lamv, g_col, lam_init=lam_init, batch=batch, seq=seq,
                                         ctx=n_ctx)

        cw = jnp.concatenate([conv_w[l], jnp.zeros((1, CONV_C), F32)], 0)
        z = _conv(y, cw, conv_b[l].reshape(1, -1), conv_ln_g[l].reshape(1, -1), conv_ln_b[l].reshape(1, -1),
                  n_rows=n_rows, n_lat=n_lat, seq=seq, ctx=n_ctx)

        x1, hext, meta, counts = _merge(
            o, o_ctx, z, gates, *stream, mod, w_attn_o[l].astype(BF16), w_conv_o[l].astype(BF16), w_out[l].astype(BF16),
            ln1_g[l].reshape(1, -1), ln1_b[l].reshape(1, -1), wr_hi, wr_lo, br,
            rb=rb, n_rows=n_rows, n_lat=n_lat, seq=seq, batch=batch, alpha=alpha)

        cnt = counts[:N_CLASSES, 0].astype(jnp.int32)
        padded = ((cnt + tm - 1) // tm) * tm
        ends = jnp.cumsum(padded)
        starts = ends - padded
        cls = meta[0, :n_rows].astype(jnp.int32)
        pos = starts[cls] + meta[1, :n_rows].astype(jnp.int32)
        pos3 = pos.reshape(n_rows // rbd, 1, rbd)
        tile_row0 = jnp.arange(p_rows // tm, dtype=jnp.int32) * tm
        tile_cls = jnp.minimum(jnp.sum(tile_row0[:, None] >= ends[None, :], axis=1), N_CLASSES - 1)
        tile_valid = (tile_row0 < ends[-1]).astype(jnp.int32)
        tile_lo, tile_hi = cls_lo[tile_cls], cls_hi[tile_cls]

        hs = _dispatch(pos3, hext, jnp.zeros((p_rows, d + META_W), F32), rb=rbd, n_rows=n_rows)
        wgu = jnp.concatenate([w_e_gate[l], w_e_up[l]], -1).astype(BF16)
        ys = _experts(tile_lo, tile_hi, tile_valid, hs, wgu, w_e_down[l].astype(BF16), tm=tm)
        x_all = _combine(pos3, ys, x1, mod, ln2_g[l].reshape(1, -1), ln2_b[l].reshape(1, -1),
                         rb=rbd, n_rows=n_rows, n_lat=n_lat, seq=seq, batch=batch, alpha=alpha)
        stream = (x_all, x_all, n_lat // rb)

    return x_all.reshape(batch, seq, d)
```

```python
import functools
import math

import jax
import jax.numpy as jnp
from jax import lax
from jax.experimental import pallas as pl
from jax.experimental.pallas import tpu as pltpu

F32 = jnp.float32
BF16 = jnp.bfloat16

D_MODEL = 1024
HEADS = 4
HEAD_DIM = 64
HEAD_W = 2 * HEAD_DIM
QK_W = HEADS * HEAD_W
CONV_C = 512
CONV_K = 31
CONV_HALO = 16
N_EXPERTS = 16
N_GROUPS = 4
EXPERTS_PER_GROUP = 4
PAIRS_PER_GROUP = 6
N_CLASSES = N_GROUPS * PAIRS_PER_GROUP
CLASS_ROWS = 32
D_EXPERT = 512
N_MOD = 6
GRID_W = 64
ROPE_AXIS_DIM = HEAD_DIM // 2
ROPE_BASE = 10000.0
EPS = 1e-5
LANES = 128
SUBLANES = 8
ROWS_PER_ISSUE = 8
META_W = LANES
NEG_BIG = -1e30

VMEM_LIMIT = 56 * 1024 * 1024


def _dot(a, b):
    return jnp.dot(a, b, preferred_element_type=F32)


def _dot_nt(a, b):
    return lax.dot_general(a, b, (((1,), (1,)), ((), ())), preferred_element_type=F32)


def _split_bf16(a):
    hi = a.astype(BF16)
    lo = (a - hi.astype(F32)).astype(BF16)
    return hi, lo


def _sigmoid(x):
    return 1.0 / (1.0 + jnp.exp(-x))


def _layer_norm(x, g, b):
    mu = jnp.mean(x, axis=-1, keepdims=True)
    xc = x - mu
    var = jnp.mean(xc * xc, axis=-1, keepdims=True)
    return xc * lax.rsqrt(var + EPS) * g + b


def _params(sem):
    return pltpu.CompilerParams(dimension_semantics=sem, vmem_limit_bytes=VMEM_LIMIT)


def _ada_kernel(c_ref, w_ref, b_ref, o_ref):
    c = c_ref[...]
    a = c * _sigmoid(c)
    a_hi, a_lo = _split_bf16(a)
    w_hi, w_lo = _split_bf16(w_ref[...])
    o_ref[...] = _dot(a_hi, w_hi) + _dot(a_lo, w_hi) + _dot(a_hi, w_lo) + b_ref[...]


def _ada(cs, w, b):
    n = w.shape[1]
    nb = 1536
    return pl.pallas_call(
        _ada_kernel,
        grid=(n // nb,),
        in_specs=[pl.BlockSpec((8, D_MODEL), lambda j: (0, 0)),
                  pl.BlockSpec((D_MODEL, nb), lambda j: (0, j)),
                  pl.BlockSpec((1, nb), lambda j: (0, j))],
        out_specs=pl.BlockSpec((8, nb), lambda j: (0, j)),
        out_shape=jax.ShapeDtypeStruct((8, n), F32),
        compiler_params=_params(("arbitrary",)),
        name="ada",
    )(cs, w, b)


def _inproj_kernel(x_ref, xc_ref, mod_ref, cos_ref, sin_ref, wqk_ref, wvt_ref, wu_ref, wg_ref,
                   q_ref, k1_ref, k2_ref, vt_ref, y_ref, g_ref, *, nlat):
    m = mod_ref[0]
    x = jnp.where(pl.program_id(0) < nlat, x_ref[...], xc_ref[...])
    hx = (x * (1.0 + m[1:2]) + m[0:1]).astype(BF16)

    qk = _dot(hx, wqk_ref[...])
    w = qk.shape[1]
    lane = lax.broadcasted_iota(jnp.int32, qk.shape, 1)
    first_half = (lane % HEAD_DIM) < (HEAD_DIM // 2)
    partner = jnp.where(first_half,
                        pltpu.roll(qk, w - HEAD_DIM // 2, 1),
                        pltpu.roll(qk, HEAD_DIM // 2, 1))
    reps = w // LANES
    cos = jnp.concatenate([cos_ref[...]] * reps, axis=1)
    sin = jnp.concatenate([sin_ref[...]] * reps, axis=1)
    roped = qk * cos + partner * sin
    q_ref[...] = (roped[:, :QK_W] * (HEAD_DIM ** -0.5 * LOG2E)).astype(BF16)
    k = roped[:, QK_W:]
    map1 = (lax.broadcasted_iota(jnp.int32, k.shape, 1) % HEAD_W) < HEAD_DIM
    k1_ref[...] = jnp.where(map1, k, 0.0).astype(BF16)
    k2_ref[...] = jnp.where(map1, 0.0, k).astype(BF16)

    vt_ref[...] = _dot_nt(wvt_ref[...], hx).astype(BF16)

    u = _dot(hx, wu_ref[...])
    y_ref[...] = u[:, :CONV_C] * _sigmoid(u[:, CONV_C:])
    g_ref[...] = _sigmoid(_dot(hx, wg_ref[...])).astype(BF16)


def _token_specs(rb, nlat, ctx_off):
    return (pl.BlockSpec((rb, D_MODEL), lambda i: (jnp.minimum(i, nlat - 1), 0)),
            pl.BlockSpec((rb, D_MODEL), lambda i: (ctx_off + jnp.maximum(i - nlat, 0), 0)))


def _inproj(x_lat, x_ctx, ctx_off, mod, cos_t, sin_t, wqk, wvt, wu, wg, *, rb, n_rows, n_lat, seq, batch):
    t = n_rows
    nlat = n_lat // rb
    per_seq = seq // rb

    def mod_idx(i):
        return (jnp.where(i < nlat, (i * rb) // seq, batch), 0, 0)

    def rope_idx(i):
        return (jnp.where(i < nlat, i % per_seq, per_seq), 0)

    const = lambda i: (0, 0)
    row = lambda i: (i, 0)
    return pl.pallas_call(
        functools.partial(_inproj_kernel, nlat=nlat),
        grid=(t // rb,),
        in_specs=[*_token_specs(rb, nlat, ctx_off),
                  pl.BlockSpec((1, N_MOD, D_MODEL), mod_idx),
                  pl.BlockSpec((rb, LANES), rope_idx),
                  pl.BlockSpec((rb, LANES), rope_idx),
                  pl.BlockSpec(wqk.shape, const),
                  pl.BlockSpec(wvt.shape, const),
                  pl.BlockSpec(wu.shape, const),
                  pl.BlockSpec(wg.shape, const)],
        out_specs=[pl.BlockSpec((rb, QK_W), row),
                   pl.BlockSpec((rb, QK_W), row),
                   pl.BlockSpec((rb, QK_W), row),
                   pl.BlockSpec((QK_W, rb), lambda i: (0, i)),
                   pl.BlockSpec((rb, CONV_C), row),
                   pl.BlockSpec((rb, 2 * D_MODEL), row)],
        out_shape=[jax.ShapeDtypeStruct((t, QK_W), BF16),
                   jax.ShapeDtypeStruct((t, QK_W), BF16),
                   jax.ShapeDtypeStruct((t, QK_W), BF16),
                   jax.ShapeDtypeStruct((QK_W, t), BF16),
                   jax.ShapeDtypeStruct((t, CONV_C), F32),
                   jax.ShapeDtypeStruct((t, 2 * D_MODEL), BF16)],
        compiler_params=_params(("arbitrary",)),
        name="inproj",
    )(x_lat, x_ctx, mod, cos_t, sin_t, wqk, wvt, wu, wg)


ONES_ROWS = 16
ACC_ROWS = HEAD_W + ONES_ROWS
LOG2E = math.log2(math.e)


def _attn_kernel(*refs, lam_init, n_chunks, kc):
    if n_chunks:
        (q_ref, kc1_ref, kc2_ref, vtc_ref, k1_ref, k2_ref, vt_ref, lam_ref, g_ref,
         o_ref, acc1_ref, acc2_ref, sa1_ref, sa2_ref, sb1_ref, sb2_ref) = refs
        key_refs = (k1_ref, k2_ref)
    else:
        (q_ref, kc1_ref, kc2_ref, vtc_ref, lam_ref, g_ref, o_ref, acc1_ref, acc2_ref) = refs
    acc_refs = (acc1_ref, acc2_ref)
    q = q_ref[...]
    qb = q.shape[0]

    def with_ones(vt):
        return jnp.concatenate([vt, jnp.ones((ONES_ROWS, vt.shape[1]), BF16)], axis=0)

    def update(i, s, cmax, m_old, vt_ext, first=False):
        m_new = jnp.maximum(m_old, cmax)
        p = jnp.exp2(s - m_new).astype(BF16)
        pv = _dot(vt_ext, p)
        if first:
            acc_refs[i][...] = pv
        else:
            acc_refs[i][...] = jnp.exp2(m_old - m_new) * acc_refs[i][...] + pv
        return m_new

    vtc_ext = with_ones(vtc_ref[...])
    m_init = jnp.full((1, qb), NEG_BIG, F32)
    n_ctx = kc1_ref.shape[0]
    ctx_scores = []
    for i, kref in enumerate((kc1_ref, kc2_ref)):
        s = _dot_nt(kref[...], q)
        cmax = jnp.max(s, axis=0, keepdims=True)
        if n_chunks:
            (sb1_ref, sb2_ref)[i][0:n_ctx, :] = s
        ctx_scores.append((s, cmax))

    if not n_chunks:
        ms = [update(i, s, cmax, m_init, vtc_ext, first=True) for i, (s, cmax) in enumerate(ctx_scores)]

    if n_chunks:
        def scores(i, c, s_refs):
            off = pl.multiple_of(c * kc, kc)
            s = _dot_nt(key_refs[i][pl.ds(off, kc), :], q)
            s_refs[i][...] = s
            return jnp.max(s, axis=0, keepdims=True)

        def values(c):
            return with_ones(vt_ref[:, pl.ds(pl.multiple_of(c * kc, kc), kc)])

        def advance(c_next, s_next, c, s_cur, cm_cur, m):
            vt_ext = values(c)
            cm_next, m = [None, None], list(m)
            for i in range(2):
                cm_next[i] = scores(i, c_next, s_next)
                m[i] = update(i, s_cur[i][...], cm_cur[i], m[i], vt_ext)
            return cm_next, m

        sa = (sa1_ref, sa2_ref)
        sb = (sb1_ref, sb2_ref)
        cm_a, ms = [None, None], [None, None]
        for i, (_, cmax) in enumerate(ctx_scores):
            cm_a[i] = scores(i, 0, sa)
            ms[i] = update(i, sb[i][0:n_ctx, :], cmax, m_init, vtc_ext, first=True)

        def body(j, carry):
            c = 2 * j
            cm_b, m = advance(c + 1, sb, c, sa, carry[2:], carry[:2])
            cm_a, m = advance(c + 2, sa, c + 1, sb, cm_b, m)
            return (*m, *cm_a)

        carry = lax.fori_loop(0, n_chunks // 2 - 1, body, (*ms, *cm_a))
        cm_b, m = advance(n_chunks - 1, sb, n_chunks - 2, sa, carry[2:], carry[:2])
        vt_ext = values(n_chunks - 1)
        for i in range(2):
            update(i, sb[i][...], cm_b[i], m[i], vt_ext)

    lv = lam_ref[...]
    lam = (jnp.exp(jnp.sum(lv[0:1] * lv[1:2], axis=1, keepdims=True))
           - jnp.exp(jnp.sum(lv[2:3] * lv[3:4], axis=1, keepdims=True)) + lam_init)
    o1 = acc1_ref[0:HEAD_W, :] / acc1_ref[HEAD_W:HEAD_W + 1, :]
    o2 = acc2_ref[0:HEAD_W, :] / acc2_ref[HEAD_W:HEAD_W + 1, :]
    o_t = o1 - lam * o2
    mean_sq = jnp.mean(o_t * o_t, axis=0, keepdims=True)
    o_t = o_t * lax.rsqrt(mean_sq + EPS) * (g_ref[...] * (1.0 - lam_init))
    o_ref[...] = o_t.T.astype(BF16)


def _attn_latent(q, k1, k2, vt, lamv, g_col, *, lam_init, batch, seq, ctx, qb, kc):
    n_lat = batch * seq
    nq = seq // qb
    ctx0 = n_lat // ctx
    n_chunks = seq // kc
    assert n_chunks >= 2 and n_chunks % 2 == 0 and ctx <= kc
    kern = functools.partial(_attn_kernel, lam_init=lam_init, n_chunks=n_chunks, kc=kc)
    return pl.pallas_call(
        kern,
        grid=(batch, HEADS, nq),
        in_specs=[pl.BlockSpec((qb, HEAD_W), lambda b, h, i: (b * nq + i, h)),
                  pl.BlockSpec((ctx, HEAD_W), lambda b, h, i: (ctx0 + b, h)),
                  pl.BlockSpec((ctx, HEAD_W), lambda b, h, i: (ctx0 + b, h)),
                  pl.BlockSpec((HEAD_W, ctx), lambda b, h, i: (h, ctx0 + b)),
                  pl.BlockSpec((seq, HEAD_W), lambda b, h, i: (b, h)),
                  pl.BlockSpec((seq, HEAD_W), lambda b, h, i: (b, h)),
                  pl.BlockSpec((HEAD_W, seq), lambda b, h, i: (h, b)),
                  pl.BlockSpec((4, HEAD_DIM), lambda b, h, i: (0, 0)),
                  pl.BlockSpec((HEAD_W, 1), lambda b, h, i: (0, 0))],
        out_specs=pl.BlockSpec((qb, HEAD_W), lambda b, h, i: (b * nq + i, h)),
        out_shape=jax.ShapeDtypeStruct((n_lat, QK_W), BF16),
        scratch_shapes=[pltpu.VMEM((ACC_ROWS, qb), F32)] * 2 + [pltpu.VMEM((kc, qb), F32)] * 4,
        compiler_params=_params(("arbitrary", "arbitrary", "arbitrary")),
        name="attn_latent",
    )(q, k1, k2, vt, k1, k2, vt, lamv, g_col)


def _attn_ctx(q, k1, k2, vt, lamv, g_col, *, lam_init, batch, seq, ctx):
    ctx0 = batch * seq // ctx
    kern = functools.partial(_attn_kernel, lam_init=lam_init, n_chunks=0, kc=0)
    return pl.pallas_call(
        kern,
        grid=(batch, HEADS),
        in_specs=[pl.BlockSpec((ctx, HEAD_W), lambda b, h: (ctx0 + b, h)),
                  pl.BlockSpec((ctx, HEAD_W), lambda b, h: (ctx0 + b, h)),
                  pl.BlockSpec((ctx, HEAD_W), lambda b, h: (ctx0 + b, h)),
                  pl.BlockSpec((HEAD_W, ctx), lambda b, h: (h, ctx0 + b)),
                  pl.BlockSpec((4, HEAD_DIM), lambda b, h: (0, 0)),
                  pl.BlockSpec((HEAD_W, 1), lambda b, h: (0, 0))],
        out_specs=pl.BlockSpec((ctx, HEAD_W), lambda b, h: (b, h)),
        out_shape=jax.ShapeDtypeStruct((batch * ctx, QK_W), BF16),
        scratch_shapes=[pltpu.VMEM((ACC_ROWS, ctx), F32)] * 2,
        compiler_params=_params(("arbitrary", "arbitrary")),
        name="attn_ctx",
    )(q, k1, k2, vt, lamv, g_col)


CONV_ROW_TILE = 64


def _conv_kernel(y_ref, prev_ref, next_ref, w_ref, b_ref, g_ref, beta_ref, z_ref, buf_ref, shifted_ref, *,
                 rb, nlat, per_seq):
    i = pl.program_id(0)
    pos = i % per_seq
    first = jnp.logical_or(i >= nlat, pos == 0)
    last = jnp.logical_or(i >= nlat, pos == per_seq - 1)

    buf_ref[CONV_HALO:CONV_HALO + rb, :] = y_ref[...]

    @pl.when(first)
    def _():
        buf_ref[0:CONV_HALO, :] = jnp.zeros((CONV_HALO, CONV_C), F32)

    @pl.when(jnp.logical_not(first))
    def _():
        buf_ref[0:CONV_HALO, :] = prev_ref[...]

    @pl.when(last)
    def _():
        buf_ref[CONV_HALO + rb:, :] = jnp.zeros((CONV_HALO, CONV_C), F32)

    @pl.when(jnp.logical_not(last))
    def _():
        buf_ref[CONV_HALO + rb:, :] = next_ref[...]

    n_sh = shifted_ref.shape[1]
    for r in range(1, SUBLANES):
        shifted_ref[r] = buf_ref[r:r + n_sh, :]

    base = CONV_HALO - CONV_K // 2
    for r0 in range(0, rb, CONV_ROW_TILE):
        acc = jnp.zeros((CONV_ROW_TILE, CONV_C), F32) + b_ref[...]
        for k in range(CONV_K):
            a, r = divmod(base + k, SUBLANES)
            lo = r0 + a * SUBLANES
            rows = buf_ref[lo:lo + CONV_ROW_TILE, :] if r == 0 else shifted_ref[r, lo:lo + CONV_ROW_TILE, :]
            acc = acc + w_ref[k:k + 1, :] * rows
        v = _layer_norm(acc, g_ref[...], beta_ref[...])
        z_ref[r0:r0 + CONV_ROW_TILE, :] = (v * _sigmoid(v)).astype(BF16)


def _conv(y, w, b, g, beta, *, n_rows, n_lat, seq, ctx):
    rb = ctx
    hb = rb // CONV_HALO
    t = y.shape[0]
    n_halo = t // CONV_HALO
    kern = functools.partial(_conv_kernel, rb=rb, nlat=n_lat // rb, per_seq=seq // rb)
    const = lambda i: (0, 0)
    return pl.pallas_call(
        kern,
        grid=(n_rows // rb,),
        in_specs=[pl.BlockSpec((rb, CONV_C), lambda i: (i, 0)),
                  pl.BlockSpec((CONV_HALO, CONV_C), lambda i: (jnp.maximum(i * hb - 1, 0), 0)),
                  pl.BlockSpec((CONV_HALO, CONV_C), lambda i: (jnp.minimum((i + 1) * hb, n_halo - 1), 0)),
                  pl.BlockSpec(w.shape, const),
                  pl.BlockSpec((1, CONV_C), const),
                  pl.BlockSpec((1, CONV_C), const),
                  pl.BlockSpec((1, CONV_C), const)],
        out_specs=pl.BlockSpec((rb, CONV_C), lambda i: (i, 0)),
        out_shape=jax.ShapeDtypeStruct((n_rows, CONV_C), BF16),
        scratch_shapes=[pltpu.VMEM((rb + 2 * CONV_HALO, CONV_C), F32),
                        pltpu.VMEM((SUBLANES, rb + 2 * CONV_HALO - SUBLANES, CONV_C), F32)],
        compiler_params=_params(("arbitrary",)),
        name="conv",
    )(y, y, y, w, b, g, beta)


def _first_index(vals, target):
    idx = jnp.full(target.shape, len(vals) - 1, jnp.int32)
    for j in range(len(vals) - 2, -1, -1):
        idx = jnp.where(vals[j] == target, j, idx)
    return idx


def _route(logits_t):
    mx = jnp.max(logits_t, axis=0, keepdims=True)
    e = jnp.exp(logits_t - mx)
    m1s, m2s, i1s, i2s, scores = [], [], [], [], []
    for g in range(N_GROUPS):
        a = [e[g * EXPERTS_PER_GROUP + j:g * EXPERTS_PER_GROUP + j + 1, :] for j in range(EXPERTS_PER_GROUP)]
        m1 = functools.reduce(jnp.maximum, a)
        i1 = _first_index(a, m1)
        rest = [jnp.where(i1 == j, -1.0, a[j]) for j in range(EXPERTS_PER_GROUP)]
        m2 = functools.reduce(jnp.maximum, rest)
        i2 = _first_index(rest, m2)
        m1s.append(m1); m2s.append(m2); i1s.append(i1); i2s.append(i2); scores.append(m1 + m2)
    gstar = _first_index(scores, functools.reduce(jnp.maximum, scores))

    def pick(xs):
        out = xs[N_GROUPS - 1]
        for g in range(N_GROUPS - 2, -1, -1):
            out = jnp.where(gstar == g, xs[g], out)
        return out

    m1, m2, i1, i2 = pick(m1s), pick(m2s), pick(i1s), pick(i2s)
    tot = m1 + m2
    w1, w2 = m1 / tot, m2 / tot
    lo = jnp.minimum(i1, i2)
    hi = jnp.maximum(i1, i2)
    w_lo = jnp.where(i1 < i2, w1, w2)
    w_hi = jnp.where(i1 < i2, w2, w1)
    pair = jnp.where(lo == 0, 0, jnp.where(lo == 1, 3, 5)) + (hi - lo - 1)
    return gstar * PAIRS_PER_GROUP + pair, w_lo, w_hi


def _merge_kernel(o_ref, octx_ref, z_ref, g_ref, x_ref, xc_ref, mod_ref, wao_ref, wco_ref, wout_ref, lng_ref, lnb_ref,
                  wr_hi_ref, wr_lo_ref, br_ref,
                  x1_ref, hext_ref, meta_ref, cnt_ref, carry_ref, *, alpha, nlat):
    i = pl.program_id(0)

    @pl.when(i == 0)
    def _():
        carry_ref[...] = jnp.zeros(carry_ref.shape, F32)

    m = mod_ref[0]
    att = _dot(jnp.where(i < nlat, o_ref[...], octx_ref[...]), wao_ref[...])
    cnv = _dot(z_ref[...], wco_ref[...])
    gates = g_ref[...].astype(F32)
    mixin = gates[:, :D_MODEL] * att + gates[:, D_MODEL:] * cnv
    mix = _dot(mixin.astype(BF16), wout_ref[...])
    x = jnp.where(i < nlat, x_ref[...], xc_ref[...])
    x1 = _layer_norm(alpha * x + m[2:3] * mix, lng_ref[...], lnb_ref[...])
    x1_ref[...] = x1
    h = x1 * (1.0 + m[4:5]) + m[3:4]
    hext_ref[:, :D_MODEL] = h

    h_hi, h_lo = _split_bf16(h)
    logits_t = (_dot_nt(wr_hi_ref[...], h_hi) + _dot_nt(wr_hi_ref[...], h_lo)
                + _dot_nt(wr_lo_ref[...], h_hi) + br_ref[...])
    cls, w_lo, w_hi = _route(logits_t)
    rb = h.shape[0]

    crow = lax.broadcasted_iota(jnp.int32, (CLASS_ROWS, rb), 0)
    onehot = (crow == cls).astype(F32)
    tri = (lax.broadcasted_iota(jnp.int32, (rb, rb), 0)
           < lax.broadcasted_iota(jnp.int32, (rb, rb), 1)).astype(BF16)
    before = _dot(onehot.astype(BF16), tri) + carry_ref[:, 0:1]
    rank = jnp.sum(onehot * before, axis=0, keepdims=True)
    carry_ref[...] = carry_ref[...] + jnp.sum(onehot, axis=1, keepdims=True)
    cnt_ref[...] = carry_ref[...]

    mrow = lax.broadcasted_iota(jnp.int32, (8, rb), 0)
    meta_ref[...] = jnp.where(mrow == 0, cls.astype(F32), jnp.where(mrow == 1, rank, 0.0))

    wrow = lax.broadcasted_iota(jnp.int32, (META_W, rb), 0)
    wts_t = jnp.where(wrow == 0, w_lo, jnp.where(wrow == 1, w_hi, 0.0))
    hext_ref[:, D_MODEL:] = wts_t.T


def _merge(o, o_ctx, z, gates, x_lat, x_ctx, ctx_off, mod, wao, wco, wout, lng, lnb, wr_hi, wr_lo, br, *,
           rb, n_rows, n_lat, seq, batch, alpha):
    t = n_rows
    nlat = n_lat // rb

    def mod_idx(i):
        return (jnp.where(i < nlat, (i * rb) // seq, batch), 0, 0)

    const = lambda i: (0, 0)
    row = lambda i: (i, 0)
    kern = functools.partial(_merge_kernel, alpha=alpha, nlat=nlat)
    return pl.pallas_call(
        kern,
        grid=(n_rows // rb,),
        in_specs=[pl.BlockSpec((rb, QK_W), lambda i: (jnp.minimum(i, nlat - 1), 0)),
                  pl.BlockSpec((rb, QK_W), lambda i: (jnp.maximum(i - nlat, 0), 0)),
                  pl.BlockSpec((rb, CONV_C), row),
                  pl.BlockSpec((rb, 2 * D_MODEL), row),
                  *_token_specs(rb, nlat, ctx_off),
                  pl.BlockSpec((1, N_MOD, D_MODEL), mod_idx),
                  pl.BlockSpec(wao.shape, const),
                  pl.BlockSpec(wco.shape, const),
                  pl.BlockSpec(wout.shape, const),
                  pl.BlockSpec((1, D_MODEL), const),
                  pl.BlockSpec((1, D_MODEL), const),
                  pl.BlockSpec(wr_hi.shape, const),
                  pl.BlockSpec(wr_lo.shape, const),
                  pl.BlockSpec((N_EXPERTS, 1), const)],
        out_specs=[pl.BlockSpec((rb, D_MODEL), row),
                   pl.BlockSpec((rb, D_MODEL + META_W), row),
                   pl.BlockSpec((8, rb), lambda i: (0, i)),
                   pl.BlockSpec((CLASS_ROWS, LANES), const)],
        out_shape=[jax.ShapeDtypeStruct((t, D_MODEL), F32),
                   jax.ShapeDtypeStruct((t, D_MODEL + META_W), F32),
                   jax.ShapeDtypeStruct((8, t), F32),
                   jax.ShapeDtypeStruct((CLASS_ROWS, LANES), F32)],
        scratch_shapes=[pltpu.VMEM((CLASS_ROWS, LANES), F32)],
        compiler_params=_params(("arbitrary",)),
        name="merge",
    )(o, o_ctx, z, gates, x_lat, x_ctx, mod, wao, wco, wout, lng, lnb, wr_hi, wr_lo, br)


def _dispatch_kernel(pos_ref, h_ref, init_ref, out_ref, buf_ref, load_sem, scat_sem, *, rb, n_steps):
    del init_ref
    i = pl.program_id(0)
    slot = i % 2

    def load(step, s):
        return pltpu.make_async_copy(h_ref.at[pl.ds(step * rb, rb)], buf_ref.at[s], load_sem.at[s])

    def wait_scatter(s):
        pltpu.make_async_copy(buf_ref.at[s], out_ref.at[pl.ds(0, rb)], scat_sem.at[s]).wait()

    @pl.when(i == 0)
    def _():
        load(0, 0).start()

    @pl.when(i > 0)
    def _():
        wait_scatter(1 - slot)

    @pl.when(i + 1 < n_steps)
    def _():
        load(i + 1, 1 - slot).start()

    load(i, slot).wait()

    def issue(g, carry):
        for u in range(ROWS_PER_ISSUE):
            r = g * ROWS_PER_ISSUE + u
            pltpu.make_async_copy(buf_ref.at[slot, pl.ds(r, 1)], out_ref.at[pl.ds(pos_ref[0, 0, r], 1)],
                                  scat_sem.at[slot]).start(priority=u % 2)
        return carry

    lax.fori_loop(0, rb // ROWS_PER_ISSUE, issue, 0)

    @pl.when(i == n_steps - 1)
    def _():
        wait_scatter(slot)


def _dispatch(pos3, hext, init, *, rb, n_rows):
    n_steps = n_rows // rb
    kern = functools.partial(_dispatch_kernel, rb=rb, n_steps=n_steps)
    return pl.pallas_call(
        kern,
        grid=(n_steps,),
        in_specs=[pl.BlockSpec((1, 1, rb), lambda i: (i, 0, 0), memory_space=pltpu.SMEM),
                  pl.BlockSpec(memory_space=pl.ANY),
                  pl.BlockSpec(memory_space=pl.ANY)],
        out_specs=pl.BlockSpec(memory_space=pl.ANY),
        out_shape=jax.ShapeDtypeStruct(init.shape, F32),
        scratch_shapes=[pltpu.VMEM((2, rb, hext.shape[1]), F32), pltpu.SemaphoreType.DMA((2,)),
                        pltpu.SemaphoreType.DMA((2,))],
        input_output_aliases={2: 0},
        compiler_params=_params(("arbitrary",)),
        name="dispatch",
    )(pos3, hext, init)


def _experts_kernel(elo_ref, ehi_ref, valid_ref, x_ref, wgu_lo_ref, wgu_hi_ref, wd_lo_ref, wd_hi_ref, y_ref):
    j = pl.program_id(0)

    @pl.when(valid_ref[j] > 0)
    def _():
        xb = x_ref[:, :D_MODEL].astype(BF16)
        wts = x_ref[:, D_MODEL:]

        def act_of(gu):
            gate = gu[:, :D_EXPERT]
            return (gate * _sigmoid(gate) * gu[:, D_EXPERT:]).astype(BF16)

        gu_lo = _dot(xb, wgu_lo_ref[0])
        gu_hi = _dot(xb, wgu_hi_ref[0])
        y_lo = _dot(act_of(gu_lo), wd_lo_ref[0])
        y_hi = _dot(act_of(gu_hi), wd_hi_ref[0])
        y_ref[...] = wts[:, 0:1] * y_lo + wts[:, 1:2] * y_hi

    @pl.when(valid_ref[j] == 0)
    def _():
        y_ref[...] = jnp.zeros(y_ref.shape, F32)


def _experts(tile_lo, tile_hi, tile_valid, hs, wgu, wd, *, tm):
    p, width = hs.shape
    grid_spec = pltpu.PrefetchScalarGridSpec(
        num_scalar_prefetch=3,
        grid=(p // tm,),
        in_specs=[pl.BlockSpec((tm, width), lambda j, lo, hi, v: (j, 0)),
                  pl.BlockSpec((1, D_MODEL, 2 * D_EXPERT), lambda j, lo, hi, v: (lo[j], 0, 0)),
                  pl.BlockSpec((1, D_MODEL, 2 * D_EXPERT), lambda j, lo, hi, v: (hi[j], 0, 0)),
                  pl.BlockSpec((1, D_EXPERT, D_MODEL), lambda j, lo, hi, v: (lo[j], 0, 0)),
                  pl.BlockSpec((1, D_EXPERT, D_MODEL), lambda j, lo, hi, v: (hi[j], 0, 0))],
        out_specs=pl.BlockSpec((tm, D_MODEL), lambda j, lo, hi, v: (j, 0)),
    )
    return pl.pallas_call(
        _experts_kernel,
        grid_spec=grid_spec,
        out_shape=jax.ShapeDtypeStruct((p, D_MODEL), F32),
        compiler_params=_params(("arbitrary",)),
        name="experts",
    )(tile_lo, tile_hi, tile_valid, hs, wgu, wgu, wd, wd)


def _combine_kernel(pos_ref, pos_next_ref, ys_ref, x1_ref, mod_ref, lng_ref, lnb_ref, out_ref, ybuf_ref, sem, *,
                    rb, alpha, n_steps):
    i = pl.program_id(0)
    slot = i % 2

    def gather(p_ref, dst_slot):
        def issue(g, carry):
            for u in range(ROWS_PER_ISSUE):
                r = g * ROWS_PER_ISSUE + u
                pltpu.make_async_copy(ys_ref.at[pl.ds(p_ref[0, 0, r], 1)], ybuf_ref.at[dst_slot, pl.ds(r, 1)],
                                      sem.at[dst_slot]).start(priority=u % 2)
            return carry
        lax.fori_loop(0, rb // ROWS_PER_ISSUE, issue, 0)

    @pl.when(i == 0)
    def _():
        gather(pos_ref, 0)

    @pl.when(i + 1 < n_steps)
    def _():
        gather(pos_next_ref, 1 - slot)

    pltpu.make_async_copy(ys_ref.at[pl.ds(0, rb)], ybuf_ref.at[slot], sem.at[slot]).wait()

    m = mod_ref[0]
    out_ref[...] = _layer_norm(alpha * x1_ref[...] + m[5:6] * ybuf_ref[slot], lng_ref[...], lnb_ref[...])


def _combine(pos3, ys, x1, mod, lng, lnb, *, rb, n_rows, n_lat, seq, batch, alpha):
    nlat = n_lat // rb

    def mod_idx(i):
        return (jnp.where(i < nlat, (i * rb) // seq, batch), 0, 0)

    const = lambda i: (0, 0)
    n_steps = n_rows // rb
    kern = functools.partial(_combine_kernel, rb=rb, alpha=alpha, n_steps=n_steps)
    return pl.pallas_call(
        kern,
        grid=(n_steps,),
        in_specs=[pl.BlockSpec((1, 1, rb), lambda i: (i, 0, 0), memory_space=pltpu.SMEM),
                  pl.BlockSpec((1, 1, rb), lambda i: (jnp.minimum(i + 1, n_steps - 1), 0, 0),
                               memory_space=pltpu.SMEM),
                  pl.BlockSpec(memory_space=pl.ANY),
                  pl.BlockSpec((rb, D_MODEL), lambda i: (i, 0)),
                  pl.BlockSpec((1, N_MOD, D_MODEL), mod_idx),
                  pl.BlockSpec((1, D_MODEL), const),
                  pl.BlockSpec((1, D_MODEL), const)],
        out_specs=pl.BlockSpec((rb, D_MODEL), lambda i: (i, 0)),
        out_shape=jax.ShapeDtypeStruct((n_rows, D_MODEL), F32),
        scratch_shapes=[pltpu.VMEM((2, rb, D_MODEL), F32), pltpu.SemaphoreType.DMA((2,))],
        compiler_params=_params(("arbitrary",)),
        name="combine",
    )(pos3, pos3, ys, x1, mod, lng, lnb)


def _rope_tables(seq, pad_rows):
    rows = seq // GRID_W
    row = jnp.repeat(jnp.arange(rows, dtype=F32), GRID_W)
    col = jnp.tile(jnp.arange(GRID_W, dtype=F32), rows)
    inv_freq = ROPE_BASE ** (-jnp.arange(0, ROPE_AXIS_DIM, 2, dtype=F32) / ROPE_AXIS_DIM)
    ang = jnp.concatenate([row[:, None] * inv_freq, col[:, None] * inv_freq], -1)
    cos, sin = jnp.cos(ang), jnp.sin(ang)
    cos64 = jnp.concatenate([cos, cos], -1)
    sin64 = jnp.concatenate([-sin, sin], -1)
    cos_t = jnp.concatenate([cos64, cos64], -1)
    sin_t = jnp.concatenate([sin64, sin64], -1)
    cos_t = jnp.concatenate([cos_t, jnp.ones((pad_rows, LANES), F32)], 0)
    sin_t = jnp.concatenate([sin_t, jnp.zeros((pad_rows, LANES), F32)], 0)
    return cos_t, sin_t


def _class_tables():
    lo, hi = [], []
    for g in range(N_GROUPS):
        for a in range(EXPERTS_PER_GROUP):
            for b in range(a + 1, EXPERTS_PER_GROUP):
                lo.append(g * EXPERTS_PER_GROUP + a)
                hi.append(g * EXPERTS_PER_GROUP + b)
    return jnp.array(lo, jnp.int32), jnp.array(hi, jnp.int32)


def kernel(x, c, ctx, c_ctx, w_ada, b_ada, w_in, lam_q1, lam_k1, lam_q2, lam_k2, subln_g, w_attn_o, conv_w, conv_b, conv_ln_g, conv_ln_b, w_conv_o, w_out, ln1_g, ln1_b, w_router, b_router, w_e_gate, w_e_up, w_e_down, ln2_g, ln2_b):
    batch, seq, d = x.shape
    n_ctx = ctx.shape[1]
    depth = w_ada.shape[0]
    n_lat = batch * seq
    t = n_lat + batch * n_ctx
    alpha = (2 * depth) ** 0.25

    rb = 512 if (seq % 512 == 0 and (batch * n_ctx) % 512 == 0) else 256
    rbd = 256
    tm = 256
    qb = 1024 if seq % 1024 == 0 else 256
    kc = 512 if seq % 512 == 0 else 256
    assert batch + 1 <= 8 and seq % rb == 0 and seq % n_ctx == 0 and n_ctx % CONV_HALO == 0

    q_end, k_end, v_end = QK_W, 2 * QK_W, 3 * QK_W
    u_end = v_end + 2 * CONV_C
    perm64 = jnp.concatenate([jnp.arange(0, HEAD_DIM, 2), jnp.arange(1, HEAD_DIM, 2)])
    perm_qk = (jnp.arange(2 * QK_W // HEAD_DIM)[:, None] * HEAD_DIM + perm64[None, :]).reshape(-1)

    cos_t, sin_t = _rope_tables(seq, rb)
    cls_lo, cls_hi = _class_tables()
    p_rows = ((t + tm - 1) // tm + N_CLASSES) * tm

    cs = jnp.zeros((8, d), F32).at[:batch].set(c).at[batch].set(c_ctx)
    stream = (x.reshape(n_lat, d), ctx.reshape(batch * n_ctx, d), 0)
    wr_hi, wr_lo = _split_bf16(w_router.T)
    br = b_router.reshape(N_EXPERTS, 1)

    for l in range(depth):
        last = l == depth - 1
        lam_init = 0.8 - 0.6 * math.exp(-0.3 * l)
        n_rows = n_lat if last else t

        mod = _ada(cs, w_ada[l], b_ada[l].reshape(1, -1)).reshape(8, N_MOD, d)

        w_l = w_in[l]
        wqk = w_l[:, :k_end][:, perm_qk].astype(BF16)
        wvt = w_l[:, k_end:v_end].T.astype(BF16)
        wu = w_l[:, v_end:u_end].astype(BF16)
        wg = w_l[:, u_end:].astype(BF16)
        q, k1, k2, vt, y, gates = _inproj(*stream, mod, cos_t, sin_t, wqk, wvt, wu, wg,
                                          rb=rb, n_rows=t, n_lat=n_lat, seq=seq, batch=batch)

        lamv = jnp.stack([lam_q1[l], lam_k1[l], lam_q2[l], lam_k2[l]])
        g_col = subln_g[l].reshape(HEAD_W, 1)
        o = _attn_latent(q, k1, k2, vt, lamv, g_col, lam_init=lam_init, batch=batch, seq=seq,
                         ctx=n_ctx, qb=qb, kc=kc)
        o_ctx = o if last else _attn_ctx(q, k1, k2, vt, lamv, g_col, lam_init=lam_init, batch=batch, seq=seq,
                                         ctx=n_ctx)

        cw = jnp.concatenate([conv_w[l], jnp.zeros((1, CONV_C), F32)], 0)
        z = _conv(y, cw, conv_b[l].reshape(1, -1), conv_ln_g[l].reshape(1, -1), conv_ln_b[l].reshape(1, -1),
                  n_rows=n_rows, n_lat=n_lat, seq=seq, ctx=n_ctx)

        x1, hext, meta, counts = _merge(
            o, o_ctx, z, gates, *stream, mod, w_attn_o[l].astype(BF16), w_conv_o[l].astype(BF16), w_out[l].astype(BF16),
            ln1_g[l].reshape(1, -1), ln1_b[l].reshape(1, -1), wr_hi, wr_lo, br,
            rb=rb, n_rows=n_rows, n_lat=n_lat, seq=seq, batch=batch, alpha=alpha)

        cnt = counts[:N_CLASSES, 0].astype(jnp.int32)
        padded = ((cnt + tm - 1) // tm) * tm
        ends = jnp.cumsum(padded)
        starts = ends - padded
        cls = meta[0, :n_rows].astype(jnp.int32)
        pos = starts[cls] + meta[1, :n_rows].astype(jnp.int32)
        pos3 = pos.reshape(n_rows // rbd, 1, rbd)
        tile_row0 = jnp.arange(p_rows // tm, dtype=jnp.int32) * tm
        tile_cls = jnp.minimum(jnp.sum(tile_row0[:, None] >= ends[None, :], axis=1), N_CLASSES - 1)
        tile_valid = (tile_row0 < ends[-1]).astype(jnp.int32)
        tile_lo, tile_hi = cls_lo[tile_cls], cls_hi[tile_cls]

        hs = _dispatch(pos3, hext, jnp.zeros((p_rows, d + META_W), F32), rb=rbd, n_rows=n_rows)
        wgu = jnp.concatenate([w_e_gate[l], w_e_up[l]], -1).astype(BF16)
        ys = _experts(tile_lo, tile_hi, tile_valid, hs, wgu, w_e_down[l].astype(BF16), tm=tm)
        x_all = _combine(pos3, ys, x1, mod, ln2_g[l].reshape(1, -1), ln2_b[l].reshape(1, -1),
                         rb=rbd, n_rows=n_rows, n_lat=n_lat, seq=seq, batch=batch, alpha=alpha)
        stream = (x_all, x_all, n_lat // rb)

    return x_all.reshape(batch, seq, d)
```
